```python
import jax
import jax.numpy as jnp
from jax import lax
import numpy as np

D_MODEL = 2048
BATCH = 2
SEQ = 16384
DEPTH = 4

CHUNK = 64
N_META = 16
N_A_LAYERS = DEPTH // 2
N_B_LAYERS = DEPTH - N_A_LAYERS
CONV_WIDTH = D_MODEL
CONV_KERNEL = 31
N_HEADS = 16
HEAD_DIM = D_MODEL // N_HEADS
ATTN_WIDTH = N_HEADS * HEAD_DIM
Q_BLOCK = 128
EPS = 1e-6

kernel_name = "yoco_conformer_conv_fox_hybrid"


def rms_norm(x, g):
    x32 = x.astype(jnp.float32)
    y = x32 * lax.rsqrt(jnp.mean(x32 * x32, axis=-1, keepdims=True) + EPS)
    return y.astype(x.dtype) * g


def layer_norm(x, g, b):
    x32 = x.astype(jnp.float32)
    mu = jnp.mean(x32, axis=-1, keepdims=True)
    xc = x32 - mu
    var = jnp.mean(xc * xc, axis=-1, keepdims=True)
    return (xc * lax.rsqrt(var + EPS)).astype(x.dtype) * g + b


def causal_depthwise_conv(u, w, b):
    y = lax.conv_general_dilated(
        u, w[:, None, :].astype(u.dtype),
        window_strides=(1,),
        padding=[(CONV_KERNEL - 1, 0)],
        dimension_numbers=("NWC", "WIO", "NWC"),
        feature_group_count=u.shape[-1])
    return y + b


def conformer_conv_mixer(h, norm_g, w_in, b_in, dw_w, dw_b, ln_g, ln_b, w_out):
    u = rms_norm(h, norm_g) @ w_in + b_in
    val, gate, z = jnp.split(u, 3, axis=-1)
    y = val * jax.nn.sigmoid(gate)
    y = causal_depthwise_conv(y, dw_w, dw_b)
    y = jax.nn.silu(layer_norm(y, ln_g, ln_b))
    return (y * jax.nn.silu(z)) @ w_out


def shared_kv(h, kv_norm_g, w_kvf, b_f):
    bsz, length, _ = h.shape
    u = rms_norm(h, kv_norm_g) @ w_kvf
    k = u[..., :ATTN_WIDTH].reshape(bsz, length, N_HEADS, HEAD_DIM)
    v = u[..., ATTN_WIDTH:2 * ATTN_WIDTH].reshape(bsz, length, N_HEADS, HEAD_DIM)
    f_logit = (u[..., 2 * ATTN_WIDTH:] + b_f).astype(jnp.float32)
    c = jnp.cumsum(jax.nn.log_sigmoid(f_logit), axis=1)
    return k, v, jnp.transpose(c, (0, 2, 1))


def forgetting_attention(q, k, v, c):
    length = q.shape[1]
    scale = HEAD_DIM ** -0.5
    outs = []
    for i in range(length // Q_BLOCK):
        qs, qe = i * Q_BLOCK, (i + 1) * Q_BLOCK
        s = jnp.einsum("bqhd,bkhd->bhqk", q[:, qs:qe], k[:, :qe]).astype(jnp.float32) * scale
        s = s + c[:, :, qs:qe, None] - c[:, :, None, :qe]
        mask = (qs + jnp.arange(Q_BLOCK))[:, None] >= jnp.arange(qe)[None, :]
        s = jnp.where(mask, s, -jnp.inf)
        p = jax.nn.softmax(s, axis=-1).astype(v.dtype)
        outs.append(jnp.einsum("bhqk,bkhd->bqhd", p, v[:, :qe]))
    return jnp.concatenate(outs, axis=1)


def fox_mixer(h, norm_g, w_in, w_out, k, v, c):
    bsz, length, _ = h.shape
    u = rms_norm(h, norm_g) @ w_in
    q, z = jnp.split(u, 2, axis=-1)
    q = q.reshape(bsz, length, N_HEADS, HEAD_DIM)
    o = forgetting_attention(q, k, v, c).reshape(bsz, length, ATTN_WIDTH)
    return (o * jax.nn.silu(z)) @ w_out


def setup_inputs(seed: int = 0) -> dict:
    key = jax.random.key(seed)
    ks = jax.random.split(key, 17)
    f32 = jnp.float32
    D, E, A = D_MODEL, CONV_WIDTH, ATTN_WIDTH
    nrm = lambda k, shape, s: jax.random.normal(k, shape, f32) * s
    return {
        "x": jax.random.normal(ks[0], (BATCH, SEQ, D), f32),
        "meta_tokens": nrm(ks[1], (N_META, D), 1.0),
        "a_norm_g": 1.0 + nrm(ks[2], (N_A_LAYERS, D), 0.02),
        "a_w_in": nrm(ks[3], (N_A_LAYERS, D, 3 * E), D ** -0.5),
        "a_b_in": nrm(ks[4], (N_A_LAYERS, 3 * E), 0.02),
        "a_dw_w": nrm(ks[5], (N_A_LAYERS, CONV_KERNEL, E), CONV_KERNEL ** -0.5),
        "a_dw_b": nrm(ks[6], (N_A_LAYERS, E), 0.02),
        "a_ln_g": 1.0 + nrm(ks[7], (N_A_LAYERS, E), 0.02),
        "a_ln_b": nrm(ks[8], (N_A_LAYERS, E), 0.02),
        "a_w_out": nrm(ks[9], (N_A_LAYERS, E, D), E ** -0.5),
        "kv_norm_g": 1.0 + nrm(ks[10], (D,), 0.02),
        "w_kvf": nrm(ks[11], (D, 2 * A + N_HEADS), D ** -0.5),
        "b_f": jax.random.uniform(ks[12], (N_HEADS,), f32, minval=1.0, maxval=5.0),
        "b_norm_g": 1.0 + nrm(ks[13], (N_B_LAYERS, D), 0.02),
        "b_w_in": nrm(ks[14], (N_B_LAYERS, D, 2 * A), D ** -0.5),
        "b_w_out": nrm(ks[15], (N_B_LAYERS, A, D), A ** -0.5),
        "final_norm_g": 1.0 + nrm(ks[16], (D,), 0.02),
    }


def reference(x, meta_tokens, a_norm_g, a_w_in, a_b_in, a_dw_w, a_dw_b, a_ln_g, a_ln_b, a_w_out,
              kv_norm_g, w_kvf, b_f, b_norm_g, b_w_in, b_w_out, final_norm_g):
    bsz, seq, d = x.shape
    total = N_META + seq
    padded = -(-total // Q_BLOCK) * Q_BLOCK
    meta = jnp.broadcast_to(meta_tokens[None].astype(x.dtype), (bsz, N_META, d))
    tail = jnp.zeros((bsz, padded - total, d), x.dtype)
    h = jnp.concatenate([meta, x, tail], axis=1)
    k = v = c = None
    for layer in range(DEPTH):
        if layer < N_A_LAYERS:
            h = h + conformer_conv_mixer(h, a_norm_g[layer], a_w_in[layer], a_b_in[layer],
                                         a_dw_w[layer], a_dw_b[layer], a_ln_g[layer],
                                         a_ln_b[layer], a_w_out[layer])
            if layer == N_A_LAYERS - 1:
                k, v, c = shared_kv(h, kv_norm_g, w_kvf, b_f)
        else:
            j = layer - N_A_LAYERS
            h = h + fox_mixer(h, b_norm_g[j], b_w_in[j], b_w_out[j], k, v, c)
    return rms_norm(h[:, N_META:N_META + seq], final_norm_g)
```

```python
import functools

import jax
import jax.numpy as jnp
from jax import lax
from jax.experimental import pallas as pl
from jax.experimental.pallas import tpu as pltpu

N_META = 16
HEAD_DIM = 128
CONV_KERNEL = 31
EPS = 1e-6

SEQ_TILE = 512
CONV_HALO = 32
BF16_SUBLANES = 16
AUG_ROWS = BF16_SUBLANES
V_ROWS = HEAD_DIM + BF16_SUBLANES
QK_DEPTH = 2 * HEAD_DIM
LOG2E = 1.4426950408889634
MASKED_KEY = -1e30
VMEM_LIMIT_BYTES = 56 * 1024 * 1024

_F32 = jnp.float32
_BF16 = jnp.bfloat16


def _dot(a, b):
    return jnp.dot(a, b, preferred_element_type=_F32)


def _dot_nt(a, b):
    return lax.dot_general(a, b, (((1,), (1,)), ((), ())), preferred_element_type=_F32)


def _rms(x, g):
    ms = jnp.mean(x * x, axis=-1, keepdims=True)
    return x * lax.rsqrt(ms + EPS) * g


def _split3(x):
    hi = x.astype(_BF16)
    r = x - hi.astype(_F32)
    mid = r.astype(_BF16)
    lo = (r - mid.astype(_F32)).astype(_BF16)
    return hi, mid, lo


def _params(n_grid_dims):
    return pltpu.CompilerParams(
        dimension_semantics=("arbitrary",) * n_grid_dims,
        vmem_limit_bytes=VMEM_LIMIT_BYTES,
    )


def _a_in_kernel(h_ref, g_ref, wv_ref, wg_ref, wz_ref, bv_ref, bg_ref, bz_ref,
                 y_ref, zs_ref, xn_ref, *, rows_per_batch, front_pad):
    i = pl.program_id(0)

    @pl.when(pl.program_id(1) == 0)
    def _():
        xn_ref[...] = _rms(h_ref[...], g_ref[...]).astype(_BF16)

    x = xn_ref[...]
    tm = x.shape[0]
    val = _dot(x, wv_ref[...]) + bv_ref[...]
    gate = _dot(x, wg_ref[...]) + bg_ref[...]
    z = _dot(x, wz_ref[...]) + bz_ref[...]
    pos = lax.rem(i * tm + lax.broadcasted_iota(jnp.int32, (tm, 1), 0), rows_per_batch)
    y = jnp.where(pos >= front_pad, val * jax.nn.sigmoid(gate), 0.0)
    y_ref[...] = y.astype(_BF16)
    zs_ref[...] = (z * jax.nn.sigmoid(z)).astype(_BF16)


def _a_in(h, g, w_in, b_in, *, tile, col_tile, rows_per_batch, front_pad):
    rows, d = h.shape
    e = w_in.shape[1] // 3
    nj = e // col_tile
    kern = functools.partial(_a_in_kernel, rows_per_batch=rows_per_batch, front_pad=front_pad)
    wspec = lambda k: pl.BlockSpec((d, col_tile), lambda i, j, k=k: (0, j + k * nj))
    bspec = lambda k: pl.BlockSpec((1, col_tile), lambda i, j, k=k: (0, j + k * nj))
    return pl.pallas_call(
        kern,
        grid=(rows // tile, nj),
        in_specs=[
            pl.BlockSpec((tile, d), lambda i, j: (i, 0)),
            pl.BlockSpec((1, d), lambda i, j: (0, 0)),
            wspec(0), wspec(1), wspec(2),
            bspec(0), bspec(1), bspec(2),
        ],
        out_specs=[
            pl.BlockSpec((tile, col_tile), lambda i, j: (i, j)),
            pl.BlockSpec((tile, col_tile), lambda i, j: (i, j)),
        ],
        out_shape=[
            jax.ShapeDtypeStruct((rows, e), _BF16),
            jax.ShapeDtypeStruct((rows, e), _BF16),
        ],
        scratch_shapes=[pltpu.VMEM((tile, d), _BF16)],
        compiler_params=_params(2),
        name="a_in",
    )(h, g, w_in, w_in, w_in, b_in, b_in, b_in)


def _a_out_kernel(y_ref, halo_ref, zs_ref, h_ref, dww_ref, dwb_ref, lng_ref, lnb_ref, w_ref,
                  o_ref, shift_ref, conv_ref, u_ref, *, conv_rows, norm_rows):
    tm, e = y_ref.shape
    for c in range(e // 128):
        cs = pl.ds(c * 128, 128)
        strip = jnp.concatenate([halo_ref[:, cs], y_ref[:, cs]], axis=0).astype(_F32)
        shift_ref[0] = strip
        for b in range(1, 8):
            shift_ref[b] = pltpu.roll(strip, b, 0)

        def conv_body(rb, carry, cs=cs):
            r0 = pl.multiple_of(rb * conv_rows, conv_rows)
            acc = jnp.broadcast_to(dwb_ref[:, cs], (conv_rows, 128))
            for k in range(CONV_KERNEL):
                d = CONV_KERNEL - 1 - k
                src = shift_ref[d % 8, pl.ds(r0 + CONV_HALO - 8 * (d // 8), conv_rows), :]
                acc = acc + dww_ref[pl.ds(k, 1), cs] * src
            conv_ref[pl.ds(r0, conv_rows), cs] = acc
            return carry

        lax.fori_loop(0, tm // conv_rows, conv_body, 0)

    def norm_body(rb, carry):
        r0 = pl.multiple_of(rb * norm_rows, norm_rows)
        rs = pl.ds(r0, norm_rows)
        v = conv_ref[rs, :]
        mu = jnp.mean(v, axis=-1, keepdims=True)
        vc = v - mu
        var = jnp.mean(vc * vc, axis=-1, keepdims=True)
        yn = vc * lax.rsqrt(var + EPS) * lng_ref[...] + lnb_ref[...]
        u = yn * jax.nn.sigmoid(yn) * zs_ref[rs, :].astype(_F32)
        u_ref[rs, :] = u.astype(_BF16)
        return carry

    lax.fori_loop(0, tm // norm_rows, norm_body, 0)
    o_ref[...] = h_ref[...] + _dot(u_ref[...], w_ref[...])


def _a_out(y, zs, h, dw_w, dw_b, ln_g, ln_b, w_out, *, tile):
    rows, e = y.shape
    d = h.shape[1]
    halo_blocks = tile // CONV_HALO
    kern = functools.partial(_a_out_kernel, conv_rows=64, norm_rows=32)
    row = lambda i: (i, 0)
    fixed = lambda i: (0, 0)
    return pl.pallas_call(
        kern,
        grid=(rows // tile,),
        in_specs=[
            pl.BlockSpec((tile, e), row),
            pl.BlockSpec((CONV_HALO, e), lambda i: (jnp.maximum(i * halo_blocks - 1, 0), 0)),
            pl.BlockSpec((tile, e), row),
            pl.BlockSpec((tile, d), row),
            pl.BlockSpec((CONV_KERNEL, e), fixed),
            pl.BlockSpec((1, e), fixed),
            pl.BlockSpec((1, e), fixed),
            pl.BlockSpec((1, e), fixed),
            pl.BlockSpec((e, d), fixed),
        ],
        out_specs=pl.BlockSpec((tile, d), row),
        out_shape=jax.ShapeDtypeStruct((rows, d), _F32),
        scratch_shapes=[
            pltpu.VMEM((8, tile + CONV_HALO, 128), _F32),
            pltpu.VMEM((tile, e), _F32),
            pltpu.VMEM((tile, e), _BF16),
        ],
        compiler_params=_params(1),
        name="a_out",
    )(y, y, zs, h, dw_w, dw_b, ln_g, ln_b, w_out)


def _kv_kernel(h_ref, g_ref, wk_ref, wvt_ref, wfh_ref, wfl_ref, bf_ref, sel_ref,
               kp_ref, vt_ref, ct_ref, xn_ref, ccol_ref, carry_ref,
               *, rows_per_batch, front_pad, heads_per_step):
    i = pl.program_id(0)
    tm = h_ref.shape[0]
    nh = wfh_ref.shape[0]
    row0 = i * tm

    @pl.when(pl.program_id(1) == 0)
    def _():
        xf = _rms(h_ref[...], g_ref[...])
        xhi = xf.astype(_BF16)
        xlo = (xf - xhi.astype(_F32)).astype(_BF16)
        xn_ref[...] = xhi
        ft = (_dot_nt(wfh_ref[...], xhi) + _dot_nt(wfh_ref[...], xlo)
              + _dot_nt(wfl_ref[...], xhi) + bf_ref[...])
        ls = jnp.minimum(ft, 0.0) - jnp.log1p(jnp.exp(-jnp.abs(ft)))
        pos = lax.rem(row0 + lax.broadcasted_iota(jnp.int32, (1, tm), 1), rows_per_batch)
        ls = jnp.where(pos >= front_pad, ls, 0.0)
        carry = jnp.where(lax.rem(row0, rows_per_batch) == 0, 0.0, carry_ref[...])
        tri = (lax.broadcasted_iota(jnp.int32, (tm, tm), 0)
               <= lax.broadcasted_iota(jnp.int32, (tm, tm), 1)).astype(_BF16)
        pieces = _dot(jnp.concatenate(_split3(ls), axis=0), tri)
        c = pieces[0:nh] + pieces[nh:2 * nh] + pieces[2 * nh:3 * nh] + carry
        carry_ref[...] = c[:, tm - 1:tm]
        hi, mid, lo = [p.astype(_F32) for p in _split3(c * LOG2E)]
        r = lax.broadcasted_iota(jnp.int32, (AUG_ROWS, tm), 0)
        ones_rows = jnp.where(r < 6, 1.0, 0.0)
        for hd in range(nh):
            blk = jnp.where(r == 0, hi[hd:hd + 1],
                            jnp.where(r == 1, mid[hd:hd + 1],
                                      jnp.where(r == 2, lo[hd:hd + 1], ones_rows)))
            ct_ref[hd] = blk.astype(_BF16)
        stack = jnp.concatenate([hi, mid, lo, jnp.zeros((128 - 3 * nh, tm), _F32)], axis=0)
        ccol_ref[...] = stack.T.astype(_BF16)

    x = xn_ref[...]
    kk = _dot(x, wk_ref[...])
    ka = _dot(ccol_ref[...], sel_ref[...])
    width = ka.shape[1]
    lane = lax.broadcasted_iota(jnp.int32, (tm, width), 1) & (HEAD_DIM - 1)
    pos = lax.rem(row0 + lax.broadcasted_iota(jnp.int32, (tm, 1), 0), rows_per_batch)
    ka = jnp.where(lane < 3, 1.0, ka)
    ka = jnp.where(jnp.logical_and(lane == 3, pos < front_pad), MASKED_KEY, ka)
    vt = _dot_nt(wvt_ref[...], x)
    ones_blk = jnp.where(lax.broadcasted_iota(jnp.int32, (BF16_SUBLANES, tm), 0) == 0, 1.0, 0.0).astype(_BF16)
    for hh in range(heads_per_step):
        hs = slice(hh * HEAD_DIM, (hh + 1) * HEAD_DIM)
        kp_ref[hh, :, 0:HEAD_DIM] = kk[:, hs].astype(_BF16)
        kp_ref[hh, :, HEAD_DIM:QK_DEPTH] = ka[:, hs].astype(_BF16)
        vt_ref[hh, 0, 0:HEAD_DIM, :] = vt[hs, :].astype(_BF16)
        vt_ref[hh, 0, HEAD_DIM:V_ROWS, :] = ones_blk


def _kv(h, g, wk, wvt, wfh, wfl, bf, sel, *, tile, heads_per_step, rows_per_batch, front_pad):
    rows, d = h.shape
    nh = wfh.shape[0]
    cw = heads_per_step * HEAD_DIM
    kern = functools.partial(_kv_kernel, rows_per_batch=rows_per_batch, front_pad=front_pad,
                             heads_per_step=heads_per_step)
    return pl.pallas_call(
        kern,
        grid=(rows // tile, nh // heads_per_step),
        in_specs=[
            pl.BlockSpec((tile, d), lambda i, j: (i, 0)),
            pl.BlockSpec((1, d), lambda i, j: (0, 0)),
            pl.BlockSpec((d, cw), lambda i, j: (0, j)),
            pl.BlockSpec((cw, d), lambda i, j: (j, 0)),
            pl.BlockSpec((nh, d), lambda i, j: (0, 0)),
            pl.BlockSpec((nh, d), lambda i, j: (0, 0)),
            pl.BlockSpec((nh, 1), lambda i, j: (0, 0)),
            pl.BlockSpec((128, cw), lambda i, j: (0, j)),
        ],
        out_specs=[
            pl.BlockSpec((heads_per_step, tile, QK_DEPTH), lambda i, j: (j, i, 0)),
            pl.BlockSpec((heads_per_step, 1, V_ROWS, tile), lambda i, j: (j, i, 0, 0)),
            pl.BlockSpec((nh, AUG_ROWS, tile), lambda i, j: (0, 0, i)),
        ],
        out_shape=[
            jax.ShapeDtypeStruct((nh, rows, QK_DEPTH), _BF16),
            jax.ShapeDtypeStruct((nh, rows // tile, V_ROWS, tile), _BF16),
            jax.ShapeDtypeStruct((nh, AUG_ROWS, rows), _BF16),
        ],
        scratch_shapes=[
            pltpu.VMEM((tile, d), _BF16),
            pltpu.VMEM((tile, 128), _BF16),
            pltpu.VMEM((nh, 1), _F32),
        ],
        compiler_params=_params(2),
        name="kv_proj",
    )(h, g, wk, wvt, wfh, wfl, bf, sel)


def _q_kernel(h_ref, g_ref, wqt_ref, wz_ref, qt_ref, zs_ref, xn_ref, *, heads_per_step, q_scale):
    @pl.when(pl.program_id(1) == 0)
    def _():
        xn_ref[...] = _rms(h_ref[...], g_ref[...]).astype(_BF16)

    x = xn_ref[...]
    qt = _dot_nt(wqt_ref[...], x) * q_scale
    for hh in range(heads_per_step):
        qt_ref[hh] = qt[hh * HEAD_DIM:(hh + 1) * HEAD_DIM, :].astype(_BF16)
    z = _dot(x, wz_ref[...])
    zs_ref[...] = (z * jax.nn.sigmoid(z)).astype(_BF16)


def _q_proj(h, g, wqt, wz, *, tile, heads_per_step):
    rows, d = h.shape
    a = wz.shape[1]
    nh = a // HEAD_DIM
    cw = heads_per_step * HEAD_DIM
    kern = functools.partial(_q_kernel, heads_per_step=heads_per_step,
                             q_scale=HEAD_DIM ** -0.5 * LOG2E)
    return pl.pallas_call(
        kern,
        grid=(rows // tile, nh // heads_per_step),
        in_specs=[
            pl.BlockSpec((tile, d), lambda i, j: (i, 0)),
            pl.BlockSpec((1, d), lambda i, j: (0, 0)),
            pl.BlockSpec((cw, d), lambda i, j: (j, 0)),
            pl.BlockSpec((d, cw), lambda i, j: (0, j)),
        ],
        out_specs=[
            pl.BlockSpec((heads_per_step, HEAD_DIM, tile), lambda i, j: (j, 0, i)),
            pl.BlockSpec((tile, cw), lambda i, j: (i, j)),
        ],
        out_shape=[
            jax.ShapeDtypeStruct((nh, HEAD_DIM, rows), _BF16),
            jax.ShapeDtypeStruct((rows, a), _BF16),
        ],
        scratch_shapes=[pltpu.VMEM((tile, d), _BF16)],
        compiler_params=_params(2),
        name="q_proj",
    )(h, g, wqt, wz)


def _attn_kernel(qt_ref, ct_ref, k_ref, vt_ref, o_ref, acc_ref, m_ref):
    qi = pl.program_id(2)
    tq = qt_ref.shape[2]
    tk = vt_ref.shape[3]
    qa = jnp.concatenate(
        [qt_ref[0], ct_ref[0], jnp.zeros((QK_DEPTH - HEAD_DIM - AUG_ROWS, tq), _BF16)], axis=0)
    m_ref[...] = jnp.full(m_ref.shape, -jnp.inf, _F32)
    acc_ref[...] = jnp.zeros(acc_ref.shape, _F32)

    def step(j, on_diagonal):
        ks = k_ref[0, pl.ds(pl.multiple_of(j * tk, tk), tk), :]
        s = _dot(ks, qa)
        if on_diagonal:
            key = lax.broadcasted_iota(jnp.int32, (tk, tq), 0)
            qry = lax.broadcasted_iota(jnp.int32, (tk, tq), 1)
            s = jnp.where(key <= qry, s, -jnp.inf)
        m_prev = m_ref[...]
        m_new = jnp.maximum(m_prev, jnp.max(s, axis=0, keepdims=True))
        p = jnp.exp2(s - m_new)
        alpha = jnp.exp2(m_prev - m_new)
        acc_ref[...] = alpha * acc_ref[...] + _dot(vt_ref[0, j], p.astype(_BF16))
        m_ref[...] = m_new

    def body(j, carry):
        step(j, False)
        return carry

    lax.fori_loop(0, qi, body, 0)
    step(qi, True)
    acc = acc_ref[...]
    o = acc[0:HEAD_DIM, :] / acc[HEAD_DIM:HEAD_DIM + 1, :]
    o_ref[...] = o.T.astype(_BF16)


def _attention(qt, ct, kp, vt, *, batch, tile):
    nh, _, rows = qt.shape
    nq = rows // tile // batch
    rows_per_batch = nq * tile
    return pl.pallas_call(
        _attn_kernel,
        grid=(nh, batch, nq),
        in_specs=[
            pl.BlockSpec((1, HEAD_DIM, tile), lambda h, b, q: (h, 0, b * nq + q)),
            pl.BlockSpec((1, AUG_ROWS, tile), lambda h, b, q: (h, 0, b * nq + q)),
            pl.BlockSpec((1, rows_per_batch, QK_DEPTH), lambda h, b, q: (h, b, 0)),
            pl.BlockSpec((1, nq, V_ROWS, tile), lambda h, b, q: (h, b, 0, 0)),
        ],
        out_specs=pl.BlockSpec((tile, HEAD_DIM), lambda h, b, q: (b * nq + q, h)),
        out_shape=jax.ShapeDtypeStruct((rows, nh * HEAD_DIM), _BF16),
        scratch_shapes=[
            pltpu.VMEM((V_ROWS, tile), _F32),
            pltpu.VMEM((1, tile), _F32),
        ],
        compiler_params=_params(3),
        name="fox_attention",
    )(qt, ct, kp, vt)


def _b_out_kernel(o_ref, zs_ref, h_ref, w_ref, out_ref):
    out_ref[...] = h_ref[...] + _dot(o_ref[...] * zs_ref[...], w_ref[...])


def _b_out_final_kernel(o_ref, zs_ref, h_ref, w_ref, g_ref, out_ref):
    out_ref[...] = _rms(h_ref[...] + _dot(o_ref[...] * zs_ref[...], w_ref[...]), g_ref[...])


def _b_out(o, zs, h, w_out, *, tile):
    rows, a = o.shape
    d = h.shape[1]
    row = lambda i: (i, 0)
    return pl.pallas_call(
        _b_out_kernel,
        grid=(rows // tile,),
        in_specs=[
            pl.BlockSpec((tile, a), row),
            pl.BlockSpec((tile, a), row),
            pl.BlockSpec((tile, d), row),
            pl.BlockSpec((a, d), lambda i: (0, 0)),
        ],
        out_specs=pl.BlockSpec((tile, d), row),
        out_shape=jax.ShapeDtypeStruct((rows, d), _F32),
        compiler_params=_params(1),
        name="b_out",
    )(o, zs, h, w_out)


def _b_out_final(o, zs, h, w_out, g, *, tile, batch, seq):
    rows, a = o.shape
    d = h.shape[1]
    tiles_per_batch = rows // batch // tile
    out_tiles_per_batch = seq // tile
    row = lambda i: (i, 0)

    def out_map(i):
        b = i // tiles_per_batch
        t = i - b * tiles_per_batch
        return (b * out_tiles_per_batch + jnp.maximum(t - (tiles_per_batch - out_tiles_per_batch), 0), 0)

    return pl.pallas_call(
        _b_out_final_kernel,
        grid=(rows // tile,),
        in_specs=[
            pl.BlockSpec((tile, a), row),
            pl.BlockSpec((tile, a), row),
            pl.BlockSpec((tile, d), row),
            pl.BlockSpec((a, d), lambda i: (0, 0)),
            pl.BlockSpec((1, d), lambda i: (0, 0)),
        ],
        out_specs=pl.BlockSpec((tile, d), out_map),
        out_shape=jax.ShapeDtypeStruct((batch * seq, d), _F32),
        compiler_params=_params(1),
        name="b_out_final",
    )(o, zs, h, w_out, g)


def _forward(x, meta_tokens, a_norm_g, a_w_in, a_b_in, a_dw_w, a_dw_b, a_ln_g, a_ln_b, a_w_out,
             kv_norm_g, w_kvf, b_f, b_norm_g, b_w_in, b_w_out, final_norm_g, *, tile):
    batch, seq, d = x.shape
    n_a = a_w_in.shape[0]
    n_b = b_w_in.shape[0]
    attn_width = b_w_out.shape[1]
    nh = attn_width // HEAD_DIM
    heads_per_step = min(4, nh)
    assert seq % tile == 0 and N_META <= tile and tile % CONV_HALO == 0
    assert nh % heads_per_step == 0 and 3 * nh <= 128
    rows_per_batch = seq + tile
    front_pad = tile - N_META
    col_tile = min(512, a_w_in.shape[2] // 3)

    row_vec = lambda v: v.reshape(1, -1).astype(_F32)
    meta = jnp.broadcast_to(meta_tokens[None].astype(x.dtype), (batch, N_META, d))
    h = jnp.concatenate([jnp.zeros((batch, front_pad, d), x.dtype), meta, x], axis=1)
    h = h.reshape(batch * rows_per_batch, d)

    for layer in range(n_a):
        y, zs = _a_in(h, row_vec(a_norm_g[layer]), a_w_in[layer].astype(_BF16), row_vec(a_b_in[layer]),
                      tile=tile, col_tile=col_tile, rows_per_batch=rows_per_batch, front_pad=front_pad)
        h = _a_out(y, zs, h, a_dw_w[layer], row_vec(a_dw_b[layer]), row_vec(a_ln_g[layer]),
                   row_vec(a_ln_b[layer]), a_w_out[layer].astype(_BF16), tile=tile)

    wk = w_kvf[:, :attn_width].astype(_BF16)
    wvt = w_kvf[:, attn_width:2 * attn_width].T.astype(_BF16)
    wft = w_kvf[:, 2 * attn_width:].T
    wfh = wft.astype(_BF16)
    wfl = (wft - wfh.astype(_F32)).astype(_BF16)
    piece = jnp.arange(128)[:, None]
    col = jnp.arange(attn_width)[None, :]
    sel = jnp.where((piece < 3 * nh) & (col // HEAD_DIM == piece % nh)
                    & (col % HEAD_DIM == 3 + piece // nh), -1.0, 0.0).astype(_BF16)
    kp, vt, ct = _kv(h, row_vec(kv_norm_g), wk, wvt, wfh, wfl, b_f.reshape(nh, 1).astype(_F32), sel,
                     tile=tile, heads_per_step=heads_per_step,
                     rows_per_batch=rows_per_batch, front_pad=front_pad)

    for layer in range(n_b):
        wqt = b_w_in[layer][:, :attn_width].T.astype(_BF16)
        wz = b_w_in[layer][:, attn_width:].astype(_BF16)
        qt, zs = _q_proj(h, row_vec(b_norm_g[layer]), wqt, wz, tile=tile, heads_per_step=heads_per_step)
        o = _attention(qt, ct, kp, vt, batch=batch, tile=tile)
        w_out = b_w_out[layer].astype(_BF16)
        if layer + 1 < n_b:
            h = _b_out(o, zs, h, w_out, tile=tile)
        else:
            h = _b_out_final(o, zs, h, w_out, row_vec(final_norm_g), tile=tile, batch=batch, seq=seq)
    return h.reshape(batch, seq, d)


def kernel(x, meta_tokens, a_norm_g, a_w_in, a_b_in, a_dw_w, a_dw_b, a_ln_g, a_ln_b, a_w_out,
           kv_norm_g, w_kvf, b_f, b_norm_g, b_w_in, b_w_out, final_norm_g):
    return _forward(x, meta_tokens, a_norm_g, a_w_in, a_b_in, a_dw_w, a_dw_b, a_ln_g, a_ln_b, a_w_out,
                    kv_norm_g, w_kvf, b_f, b_norm_g, b_w_in, b_w_out, final_norm_g, tile=SEQ_TILE)
```

```python
import functools

import jax
import jax.numpy as jnp
from jax import lax
from jax.experimental import pallas as pl
from jax.experimental.pallas import tpu as pltpu

N_META = 16
HEAD_DIM = 128
CONV_KERNEL = 31
EPS = 1e-6

SEQ_TILE = 512
A_IN_ROWS = 1024
A_OUT_ROWS = 256
CONV_HALO = 32
BF16_SUBLANES = 16
AUG_ROWS = BF16_SUBLANES
V_ROWS = HEAD_DIM + BF16_SUBLANES
QK_DEPTH = 2 * HEAD_DIM
LOG2E = 1.4426950408889634
MASKED_KEY = -1e30
VMEM_LIMIT_BYTES = 56 * 1024 * 1024

_F32 = jnp.float32
_BF16 = jnp.bfloat16


def _dot(a, b):
    return jnp.dot(a, b, preferred_element_type=_F32)


def _dot_nt(a, b):
    return lax.dot_general(a, b, (((1,), (1,)), ((), ())), preferred_element_type=_F32)


def _rms(x):
    ms = jnp.mean(x * x, axis=-1, keepdims=True)
    return x * lax.rsqrt(ms + EPS)


def _split3(x):
    hi = x.astype(_BF16)
    r = x - hi.astype(_F32)
    mid = r.astype(_BF16)
    lo = (r - mid.astype(_F32)).astype(_BF16)
    return hi, mid, lo


def _params(n_grid_dims):
    return pltpu.CompilerParams(
        dimension_semantics=("arbitrary",) * n_grid_dims,
        vmem_limit_bytes=VMEM_LIMIT_BYTES,
    )


def _resident(shape):
    return pl.BlockSpec(shape, lambda *_: (0,) * len(shape), pipeline_mode=pl.Buffered(1))


def _emit_norms(hn, gain_refs, norm_refs, lo_ref):
    r = _rms(hn)
    for k, (g_ref, n_ref) in enumerate(zip(gain_refs, norm_refs)):
        xf = r * g_ref[...]
        hi = xf.astype(_BF16)
        n_ref[...] = hi
        if k == 0 and lo_ref is not None:
            lo_ref[...] = (xf - hi.astype(_F32)).astype(_BF16)


def _embed_kernel(x_ref, meta_ref, g_ref, h_ref, xn_ref):
    t = pl.program_id(1)
    tile = h_ref.shape[0]

    @pl.when(t == 0)
    def _():
        pad = tile - N_META
        meta = meta_ref[...]
        h_ref[0:pad, :] = jnp.zeros((pad, h_ref.shape[1]), _F32)
        h_ref[pad:, :] = meta
        xn_ref[0:pad, :] = jnp.zeros((pad, h_ref.shape[1]), _BF16)
        xn_ref[pad:, :] = (_rms(meta) * g_ref[...]).astype(_BF16)

    @pl.when(t > 0)
    def _():
        x = x_ref[0]
        h_ref[...] = x
        xn_ref[...] = (_rms(x) * g_ref[...]).astype(_BF16)


def _embed(x, meta_tokens, g, *, tile):
    batch, seq, d = x.shape
    nt = seq // tile + 1
    return pl.pallas_call(
        _embed_kernel,
        grid=(batch, nt),
        in_specs=[
            pl.BlockSpec((1, tile, d), lambda b, t: (b, jnp.maximum(t - 1, 0), 0)),
            pl.BlockSpec((N_META, d), lambda b, t: (0, 0)),
            pl.BlockSpec((1, d), lambda b, t: (0, 0)),
        ],
        out_specs=[
            pl.BlockSpec((tile, d), lambda b, t: (b * nt + t, 0)),
            pl.BlockSpec((tile, d), lambda b, t: (b * nt + t, 0)),
        ],
        out_shape=[
            jax.ShapeDtypeStruct((batch * nt * tile, d), _F32),
            jax.ShapeDtypeStruct((batch * nt * tile, d), _BF16),
        ],
        compiler_params=_params(2),
        name="embed",
    )(x, meta_tokens, g)


def _a_in_kernel(xn_ref, wv_ref, wg_ref, wz_ref, bv_ref, bg_ref, bz_ref, y_ref, zs_ref,
                 *, rows_per_batch, front_pad):
    i = pl.program_id(0)
    x = xn_ref[...]
    tm = x.shape[0]
    val = _dot(x, wv_ref[...]) + bv_ref[...]
    gate = _dot(x, wg_ref[...]) + bg_ref[...]
    z = _dot(x, wz_ref[...]) + bz_ref[...]
    pos = lax.rem(i * tm + lax.broadcasted_iota(jnp.int32, (tm, 1), 0), rows_per_batch)
    y = jnp.where(pos >= front_pad, val * jax.nn.sigmoid(gate), 0.0)
    y_ref[...] = y.astype(_BF16)
    zs_ref[...] = (z * jax.nn.sigmoid(z)).astype(_BF16)


def _a_in(xn, w_in, b_in, *, tile, col_tile, rows_per_batch, front_pad):
    rows, d = xn.shape
    e = w_in.shape[1] // 3
    nj = e // col_tile
    kern = functools.partial(_a_in_kernel, rows_per_batch=rows_per_batch, front_pad=front_pad)
    wspec = lambda k: pl.BlockSpec((d, col_tile), lambda i, j, k=k: (0, j + k * nj))
    bspec = lambda k: pl.BlockSpec((1, col_tile), lambda i, j, k=k: (0, j + k * nj))
    return pl.pallas_call(
        kern,
        grid=(rows // tile, nj),
        in_specs=[
            pl.BlockSpec((tile, d), lambda i, j: (i, 0)),
            wspec(0), wspec(1), wspec(2),
            bspec(0), bspec(1), bspec(2),
        ],
        out_specs=[
            pl.BlockSpec((tile, col_tile), lambda i, j: (i, j)),
            pl.BlockSpec((tile, col_tile), lambda i, j: (i, j)),
        ],
        out_shape=[
            jax.ShapeDtypeStruct((rows, e), _BF16),
            jax.ShapeDtypeStruct((rows, e), _BF16),
        ],
        compiler_params=_params(2),
        name="a_in",
    )(xn, w_in, w_in, w_in, b_in, b_in, b_in)


def _a_out_kernel(*refs, n_norms, emit_lo, conv_rows, norm_rows):
    (y_ref, halo_ref, zs_ref, h_ref, dww_ref, dwb_ref, lng_ref, lnb_ref, w_ref), refs = refs[:9], refs[9:]
    gain_refs, refs = refs[:n_norms], refs[n_norms:]
    o_ref, refs = refs[0], refs[1:]
    norm_refs, refs = refs[:n_norms], refs[n_norms:]
    lo_ref = None
    if emit_lo:
        lo_ref, refs = refs[0], refs[1:]
    shift_ref, conv_ref, u_ref = refs

    tm, e = y_ref.shape
    for c in range(e // 128):
        cs = pl.ds(c * 128, 128)
        strip = jnp.concatenate([halo_ref[:, cs], y_ref[:, cs]], axis=0).astype(_F32)
        shift_ref[0] = strip
        for b in range(1, 8):
            shift_ref[b] = pltpu.roll(strip, b, 0)

        def conv_body(rb, carry, cs=cs):
            r0 = pl.multiple_of(rb * conv_rows, conv_rows)
            acc = jnp.broadcast_to(dwb_ref[:, cs], (conv_rows, 128))
            for k in range(CONV_KERNEL):
                d = CONV_KERNEL - 1 - k
                src = shift_ref[d % 8, pl.ds(r0 + CONV_HALO - 8 * (d // 8), conv_rows), :]
                acc = acc + dww_ref[pl.ds(k, 1), cs] * src
            conv_ref[pl.ds(r0, conv_rows), cs] = acc
            return carry

        lax.fori_loop(0, tm // conv_rows, conv_body, 0)

    def norm_body(rb, carry):
        r0 = pl.multiple_of(rb * norm_rows, norm_rows)
        rs = pl.ds(r0, norm_rows)
        v = conv_ref[rs, :]
        mu = jnp.mean(v, axis=-1, keepdims=True)
        vc = v - mu
        var = jnp.mean(vc * vc, axis=-1, keepdims=True)
        yn = vc * lax.rsqrt(var + EPS) * lng_ref[...] + lnb_ref[...]
        u = yn * jax.nn.sigmoid(yn) * zs_ref[rs, :].astype(_F32)
        u_ref[rs, :] = u.astype(_BF16)
        return carry

    lax.fori_loop(0, tm // norm_rows, norm_body, 0)
    hn = h_ref[...] + _dot(u_ref[...], w_ref[...])
    o_ref[...] = hn
    _emit_norms(hn, gain_refs, norm_refs, lo_ref)


def _a_out(y, zs, h, dw_w, dw_b, ln_g, ln_b, w_out, gains, *, tile, emit_lo):
    rows, e = y.shape
    d = h.shape[1]
    halo_blocks = tile // CONV_HALO
    n_norms = len(gains)
    kern = functools.partial(_a_out_kernel, n_norms=n_norms, emit_lo=emit_lo, conv_rows=64, norm_rows=32)
    row = lambda i: (i, 0)
    n_bf16_out = n_norms + (1 if emit_lo else 0)
    return pl.pallas_call(
        kern,
        grid=(rows // tile,),
        in_specs=[
            pl.BlockSpec((tile, e), row),
            pl.BlockSpec((CONV_HALO, e), lambda i: (jnp.maximum(i * halo_blocks - 1, 0), 0)),
            pl.BlockSpec((tile, e), row),
            pl.BlockSpec((tile, d), row),
            _resident((CONV_KERNEL, e)),
            _resident((1, e)),
            _resident((1, e)),
            _resident((1, e)),
            _resident((e, d)),
        ] + [_resident((1, d))] * n_norms,
        out_specs=[pl.BlockSpec((tile, d), row)] * (1 + n_bf16_out),
        out_shape=[jax.ShapeDtypeStruct((rows, d), _F32)]
        + [jax.ShapeDtypeStruct((rows, d), _BF16)] * n_bf16_out,
        scratch_shapes=[
            pltpu.VMEM((8, tile + CONV_HALO, 128), _F32),
            pltpu.VMEM((tile, e), _F32),
            pltpu.VMEM((tile, e), _BF16),
        ],
        compiler_params=_params(1),
        name="a_out",
    )(y, y, zs, h, dw_w, dw_b, ln_g, ln_b, w_out, *gains)


def _kv_kernel(xn_ref, xlo_ref, wk_ref, wvt_ref, wfh_ref, wfl_ref, bf_ref, sel_ref,
               kp_ref, vt_ref, ct_ref, ccol_ref, carry_ref,
               *, rows_per_batch, front_pad, heads_per_step):
    i = pl.program_id(0)
    tm = xn_ref.shape[0]
    nh = wfh_ref.shape[0]
    row0 = i * tm
    x = xn_ref[...]

    @pl.when(pl.program_id(1) == 0)
    def _():
        ft = (_dot_nt(wfh_ref[...], x) + _dot_nt(wfh_ref[...], xlo_ref[...])
              + _dot_nt(wfl_ref[...], x) + bf_ref[...])
        ls = jnp.minimum(ft, 0.0) - jnp.log1p(jnp.exp(-jnp.abs(ft)))
        pos = lax.rem(row0 + lax.broadcasted_iota(jnp.int32, (1, tm), 1), rows_per_batch)
        ls = jnp.where(pos >= front_pad, ls, 0.0)
        carry = jnp.where(lax.rem(row0, rows_per_batch) == 0, 0.0, carry_ref[...])
        tri = (lax.broadcasted_iota(jnp.int32, (tm, tm), 0)
               <= lax.broadcasted_iota(jnp.int32, (tm, tm), 1)).astype(_BF16)
        pieces = _dot(jnp.concatenate(_split3(ls), axis=0), tri)
        c = pieces[0:nh] + pieces[nh:2 * nh] + pieces[2 * nh:3 * nh] + carry
        carry_ref[...] = c[:, tm - 1:tm]
        hi, mid, lo = [p.astype(_F32) for p in _split3(c * LOG2E)]
        r = lax.broadcasted_iota(jnp.int32, (AUG_ROWS, tm), 0)
        ones_rows = jnp.where(r < 6, 1.0, 0.0)
        for hd in range(nh):
            blk = jnp.where(r == 0, hi[hd:hd + 1],
                            jnp.where(r == 1, mid[hd:hd + 1],
                                      jnp.where(r == 2, lo[hd:hd + 1], ones_rows)))
            ct_ref[hd] = blk.astype(_BF16)
        stack = jnp.concatenate([hi, mid, lo, jnp.zeros((128 - 3 * nh, tm), _F32)], axis=0)
        ccol_ref[...] = stack.T.astype(_BF16)

    kk = _dot(x, wk_ref[...])
    ka = _dot(ccol_ref[...], sel_ref[...])
    width = ka.shape[1]
    lane = lax.broadcasted_iota(jnp.int32, (tm, width), 1) & (HEAD_DIM - 1)
    pos = lax.rem(row0 + lax.broadcasted_iota(jnp.int32, (tm, 1), 0), rows_per_batch)
    ka = jnp.where(lane < 3, 1.0, ka)
    ka = jnp.where(jnp.logical_and(lane == 3, pos < front_pad), MASKED_KEY, ka)
    vt = _dot_nt(wvt_ref[...], x)
    ones_blk = jnp.where(lax.broadcasted_iota(jnp.int32, (BF16_SUBLANES, tm), 0) == 0, 1.0, 0.0).astype(_BF16)
    for hh in range(heads_per_step):
        hs = slice(hh * HEAD_DIM, (hh + 1) * HEAD_DIM)
        kp_ref[hh, :, 0:HEAD_DIM] = kk[:, hs].astype(_BF16)
        kp_ref[hh, :, HEAD_DIM:QK_DEPTH] = ka[:, hs].astype(_BF16)
        vt_ref[hh, 0, 0:HEAD_DIM, :] = vt[hs, :].astype(_BF16)
        vt_ref[hh, 0, HEAD_DIM:V_ROWS, :] = ones_blk


def _kv(xn, xlo, wk, wvt, wfh, wfl, bf, sel, *, tile, heads_per_step, rows_per_batch, front_pad):
    rows, d = xn.shape
    nh = wfh.shape[0]
    cw = heads_per_step * HEAD_DIM
    kern = functools.partial(_kv_kernel, rows_per_batch=rows_per_batch, front_pad=front_pad,
                             heads_per_step=heads_per_step)
    return pl.pallas_call(
        kern,
        grid=(rows // tile, nh // heads_per_step),
        in_specs=[
            pl.BlockSpec((tile, d), lambda i, j: (i, 0)),
            pl.BlockSpec((tile, d), lambda i, j: (i, 0)),
            pl.BlockSpec((d, cw), lambda i, j: (0, j)),
            pl.BlockSpec((cw, d), lambda i, j: (j, 0)),
            pl.BlockSpec((nh, d), lambda i, j: (0, 0)),
            pl.BlockSpec((nh, d), lambda i, j: (0, 0)),
            pl.BlockSpec((nh, 1), lambda i, j: (0, 0)),
            pl.BlockSpec((128, cw), lambda i, j: (0, j)),
        ],
        out_specs=[
            pl.BlockSpec((heads_per_step, tile, QK_DEPTH), lambda i, j: (j, i, 0)),
            pl.BlockSpec((heads_per_step, 1, V_ROWS, tile), lambda i, j: (j, i, 0, 0)),
            pl.BlockSpec((nh, AUG_ROWS, tile), lambda i, j: (0, 0, i)),
        ],
        out_shape=[
            jax.ShapeDtypeStruct((nh, rows, QK_DEPTH), _BF16),
            jax.ShapeDtypeStruct((nh, rows // tile, V_ROWS, tile), _BF16),
            jax.ShapeDtypeStruct((nh, AUG_ROWS, rows), _BF16),
        ],
        scratch_shapes=[
            pltpu.VMEM((tile, 128), _BF16),
            pltpu.VMEM((nh, 1), _F32),
        ],
        compiler_params=_params(2),
        name="kv_proj",
    )(xn, xlo, wk, wvt, wfh, wfl, bf, sel)


def _q_kernel(xn_ref, wqt_ref, wz_ref, qt_ref, zs_ref, *, heads_per_step, q_scale):
    x = xn_ref[...]
    qt = _dot_nt(wqt_ref[...], x) * q_scale
    for hh in range(heads_per_step):
        qt_ref[hh] = qt[hh * HEAD_DIM:(hh + 1) * HEAD_DIM, :].astype(_BF16)
    z = _dot(x, wz_ref[...])
    zs_ref[...] = (z * jax.nn.sigmoid(z)).astype(_BF16)


def _q_proj(xn, wqt, wz, *, tile, heads_per_step):
    rows, d = xn.shape
    a = wz.shape[1]
    nh = a // HEAD_DIM
    cw = heads_per_step * HEAD_DIM
    kern = functools.partial(_q_kernel, heads_per_step=heads_per_step,
                             q_scale=HEAD_DIM ** -0.5 * LOG2E)
    return pl.pallas_call(
        kern,
        grid=(rows // tile, nh // heads_per_step),
        in_specs=[
            pl.BlockSpec((tile, d), lambda i, j: (i, 0)),
            pl.BlockSpec((cw, d), lambda i, j: (j, 0)),
            pl.BlockSpec((d, cw), lambda i, j: (0, j)),
        ],
        out_specs=[
            pl.BlockSpec((heads_per_step, HEAD_DIM, tile), lambda i, j: (j, 0, i)),
            pl.BlockSpec((tile, cw), lambda i, j: (i, j)),
        ],
        out_shape=[
            jax.ShapeDtypeStruct((nh, HEAD_DIM, rows), _BF16),
            jax.ShapeDtypeStruct((rows, a), _BF16),
        ],
        compiler_params=_params(2),
        name="q_proj",
    )(xn, wqt, wz)


def _attn_kernel(qt_ref, ct_ref, k_ref, vt_ref, o_ref, acc_ref, m_ref, sa_ref, sb_ref):
    qi = pl.program_id(2)
    tq = qt_ref.shape[2]
    tk = vt_ref.shape[3]
    qa = jnp.concatenate(
        [qt_ref[0], ct_ref[0], jnp.zeros((QK_DEPTH - HEAD_DIM - AUG_ROWS, tq), _BF16)], axis=0)
    m_ref[...] = jnp.full(m_ref.shape, -jnp.inf, _F32)
    acc_ref[...] = jnp.zeros(acc_ref.shape, _F32)

    def logits(j, s_ref):
        ks = k_ref[0, pl.ds(pl.multiple_of(j * tk, tk), tk), :]
        s_ref[...] = _dot(ks, qa)

    def consume(j, s_ref, on_diagonal):
        s = s_ref[...]
        if on_diagonal:
            key = lax.broadcasted_iota(jnp.int32, (tk, tq), 0)
            qry = lax.broadcasted_iota(jnp.int32, (tk, tq), 1)
            s = jnp.where(key <= qry, s, -jnp.inf)
        m_prev = m_ref[...]
        m_new = jnp.maximum(m_prev, jnp.max(s, axis=0, keepdims=True))
        p = jnp.exp2(s - m_new)
        alpha = jnp.exp2(m_prev - m_new)
        acc_ref[...] = alpha * acc_ref[...] + _dot(vt_ref[0, j], p.astype(_BF16))
        m_ref[...] = m_new

    logits(0, sa_ref)

    def pair(j):
        logits(j + 1, sb_ref)
        consume(j, sa_ref, False)
        logits(j + 2, sa_ref)
        consume(j + 1, sb_ref, False)

    def quad(jj, carry):
        pair(4 * jj)
        pair(4 * jj + 2)
        return carry

    n_quads = lax.shift_right_logical(qi, 2)
    lax.fori_loop(0, n_quads, quad, 0)
    done = 4 * n_quads

    @pl.when((qi & 2) != 0)
    def _():
        pair(done)

    @pl.when((qi & 1) != 0)
    def _():
        logits(qi, sb_ref)
        consume(qi - 1, sa_ref, False)
        consume(qi, sb_ref, True)

    @pl.when((qi & 1) == 0)
    def _():
        consume(qi, sa_ref, True)

    acc = acc_ref[...]
    o = acc[0:HEAD_DIM, :] / acc[HEAD_DIM:HEAD_DIM + 1, :]
    o_ref[...] = o.T.astype(_BF16)


def _attention(qt, ct, kp, vt, *, batch, tile):
    nh, _, rows = qt.shape
    nq = rows // tile // batch
    rows_per_batch = nq * tile
    return pl.pallas_call(
        _attn_kernel,
        grid=(nh, batch, nq),
        in_specs=[
            pl.BlockSpec((1, HEAD_DIM, tile), lambda h, b, q: (h, 0, b * nq + q)),
            pl.BlockSpec((1, AUG_ROWS, tile), lambda h, b, q: (h, 0, b * nq + q)),
            pl.BlockSpec((1, rows_per_batch, QK_DEPTH), lambda h, b, q: (h, b, 0)),
            pl.BlockSpec((1, nq, V_ROWS, tile), lambda h, b, q: (h, b, 0, 0)),
        ],
        out_specs=pl.BlockSpec((tile, HEAD_DIM), lambda h, b, q: (b * nq + q, h)),
        out_shape=jax.ShapeDtypeStruct((rows, nh * HEAD_DIM), _BF16),
        scratch_shapes=[
            pltpu.VMEM((V_ROWS, tile), _F32),
            pltpu.VMEM((1, tile), _F32),
            pltpu.VMEM((tile, tile), _F32),
            pltpu.VMEM((tile, tile), _F32),
        ],
        compiler_params=_params(3),
        name="fox_attention",
    )(qt, ct, kp, vt)


def _b_out_kernel(o_ref, zs_ref, h_ref, w_ref, g_ref, out_ref, xn_ref):
    hn = h_ref[...] + _dot(o_ref[...] * zs_ref[...], w_ref[...])
    out_ref[...] = hn
    _emit_norms(hn, [g_ref], [xn_ref], None)


def _b_out_final_kernel(o_ref, zs_ref, h_ref, w_ref, g_ref, out_ref):
    hn = h_ref[...] + _dot(o_ref[...] * zs_ref[...], w_ref[...])
    out_ref[...] = _rms(hn) * g_ref[...]


def _b_out(o, zs, h, w_out, g, *, tile):
    rows, a = o.shape
    d = h.shape[1]
    row = lambda i: (i, 0)
    return pl.pallas_call(
        _b_out_kernel,
        grid=(rows // tile,),
        in_specs=[
            pl.BlockSpec((tile, a), row),
            pl.BlockSpec((tile, a), row),
            pl.BlockSpec((tile, d), row),
            _resident((a, d)),
            _resident((1, d)),
        ],
        out_specs=[pl.BlockSpec((tile, d), row), pl.BlockSpec((tile, d), row)],
        out_shape=[jax.ShapeDtypeStruct((rows, d), _F32), jax.ShapeDtypeStruct((rows, d), _BF16)],
        compiler_params=_params(1),
        name="b_out",
    )(o, zs, h, w_out, g)


def _b_out_final(o, zs, h, w_out, g, *, tile, batch, seq):
    rows, a = o.shape
    d = h.shape[1]
    tiles_per_batch = rows // batch // tile
    out_tiles_per_batch = seq // tile
    row = lambda i: (i, 0)

    def out_map(i):
        b = i // tiles_per_batch
        t = i - b * tiles_per_batch
        return (b * out_tiles_per_batch + jnp.maximum(t - (tiles_per_batch - out_tiles_per_batch), 0), 0)

    return pl.pallas_call(
        _b_out_final_kernel,
        grid=(rows // tile,),
        in_specs=[
            pl.BlockSpec((tile, a), row),
            pl.BlockSpec((tile, a), row),
            pl.BlockSpec((tile, d), row),
            _resident((a, d)),
            _resident((1, d)),
        ],
        out_specs=pl.BlockSpec((tile, d), out_map),
        out_shape=jax.ShapeDtypeStruct((batch * seq, d), _F32),
        compiler_params=_params(1),
        name="b_out_final",
    )(o, zs, h, w_out, g)


def _forward(x, meta_tokens, a_norm_g, a_w_in, a_b_in, a_dw_w, a_dw_b, a_ln_g, a_ln_b, a_w_out,
             kv_norm_g, w_kvf, b_f, b_norm_g, b_w_in, b_w_out, final_norm_g,
             *, tile, a_in_rows, a_out_rows):
    batch, seq, d = x.shape
    n_a = a_w_in.shape[0]
    n_b = b_w_in.shape[0]
    attn_width = b_w_out.shape[1]
    nh = attn_width // HEAD_DIM
    heads_per_step = min(4, nh)
    assert seq % tile == 0 and N_META <= tile and a_out_rows % CONV_HALO == 0
    assert nh % heads_per_step == 0 and 3 * nh <= 128 and n_a >= 1 and n_b >= 1
    rows_per_batch = seq + tile
    front_pad = tile - N_META
    rows = batch * rows_per_batch
    assert rows % a_in_rows == 0 and rows % a_out_rows == 0
    col_tile = min(512, a_w_in.shape[2] // 3)

    row_vec = lambda v: v.reshape(1, -1).astype(_F32)
    x = x.astype(_F32)
    h, xn = _embed(x, meta_tokens.astype(_F32), row_vec(a_norm_g[0]), tile=tile)

    xn_kv = xlo_kv = None
    for layer in range(n_a):
        y, zs = _a_in(xn, a_w_in[layer].astype(_BF16), row_vec(a_b_in[layer]), tile=a_in_rows,
                      col_tile=col_tile, rows_per_batch=rows_per_batch, front_pad=front_pad)
        last = layer + 1 == n_a
        gains = [row_vec(kv_norm_g), row_vec(b_norm_g[0])] if last else [row_vec(a_norm_g[layer + 1])]
        outs = _a_out(y, zs, h, a_dw_w[layer].astype(_F32), row_vec(a_dw_b[layer]), row_vec(a_ln_g[layer]),
                      row_vec(a_ln_b[layer]), a_w_out[layer].astype(_BF16), gains,
                      tile=a_out_rows, emit_lo=last)
        if last:
            h, xn_kv, xn, xlo_kv = outs
        else:
            h, xn = outs

    wk = w_kvf[:, :attn_width].astype(_BF16)
    wvt = w_kvf[:, attn_width:2 * attn_width].T.astype(_BF16)
    wft = w_kvf[:, 2 * attn_width:].T.astype(_F32)
    wfh = wft.astype(_BF16)
    wfl = (wft - wfh.astype(_F32)).astype(_BF16)
    piece = jnp.arange(128)[:, None]
    col = jnp.arange(attn_width)[None, :]
    sel = jnp.where((piece < 3 * nh) & (col // HEAD_DIM == piece % nh)
                    & (col % HEAD_DIM == 3 + piece // nh), -1.0, 0.0).astype(_BF16)
    kp, vt, ct = _kv(xn_kv, xlo_kv, wk, wvt, wfh, wfl, b_f.reshape(nh, 1).astype(_F32), sel,
                     tile=tile, heads_per_step=heads_per_step,
                     rows_per_batch=rows_per_batch, front_pad=front_pad)

    for layer in range(n_b):
        wqt = b_w_in[layer][:, :attn_width].T.astype(_BF16)
        wz = b_w_in[layer][:, attn_width:].astype(_BF16)
        qt, zs = _q_proj(xn, wqt, wz, tile=tile, heads_per_step=heads_per_step)
        o = _attention(qt, ct, kp, vt, batch=batch, tile=tile)
        w_out = b_w_out[layer].astype(_BF16)
        if layer + 1 < n_b:
            h, xn = _b_out(o, zs, h, w_out, row_vec(b_norm_g[layer + 1]), tile=tile)
        else:
            h = _b_out_final(o, zs, h, w_out, row_vec(final_norm_g), tile=tile, batch=batch, seq=seq)
    return h.reshape(batch, seq, d)


def kernel(x, meta_tokens, a_norm_g, a_w_in, a_b_in, a_dw_w, a_dw_b, a_ln_g, a_ln_b, a_w_out,
           kv_norm_g, w_kvf, b_f, b_norm_g, b_w_in, b_w_out, final_norm_g):
    return _forward(x, meta_tokens, a_norm_g, a_w_in, a_b_in, a_dw_w, a_dw_b, a_ln_g, a_ln_b, a_w_out,
                    kv_norm_g, w_kvf, b_f, b_norm_g, b_w_in, b_w_out, final_norm_g,
                    tile=SEQ_TILE, a_in_rows=A_IN_ROWS, a_out_rows=A_OUT_ROWS)
```

```python
import functools

import jax
import jax.numpy as jnp
from jax import lax
from jax.experimental import pallas as pl
from jax.experimental.pallas import tpu as pltpu

N_META = 16
HEAD_DIM = 128
CONV_KERNEL = 31
EPS = 1e-6

SEQ_TILE = 512
A_IN_ROWS = 1024
A_OUT_ROWS = 256
A_CONV_ROWS = 128
A_NORM_ROWS = 256
ATTN_PAIRS_PER_TRIP = 4
CONV_HALO = 32
BF16_SUBLANES = 16
AUG_ROWS = BF16_SUBLANES
V_ROWS = HEAD_DIM + BF16_SUBLANES
QK_DEPTH = 2 * HEAD_DIM
LOG2E = 1.4426950408889634
MASKED_KEY = -1e30
VMEM_LIMIT_BYTES = 56 * 1024 * 1024

_F32 = jnp.float32
_BF16 = jnp.bfloat16


def _dot(a, b):
    return jnp.dot(a, b, preferred_element_type=_F32)


def _dot_nt(a, b):
    return lax.dot_general(a, b, (((1,), (1,)), ((), ())), preferred_element_type=_F32)


def _rms(x):
    ms = jnp.mean(x * x, axis=-1, keepdims=True)
    return x * lax.rsqrt(ms + EPS)


def _split3(x):
    hi = x.astype(_BF16)
    r = x - hi.astype(_F32)
    mid = r.astype(_BF16)
    lo = (r - mid.astype(_F32)).astype(_BF16)
    return hi, mid, lo


def _params(n_grid_dims):
    return pltpu.CompilerParams(
        dimension_semantics=("arbitrary",) * n_grid_dims,
        vmem_limit_bytes=VMEM_LIMIT_BYTES,
    )


def _resident(shape):
    return pl.BlockSpec(shape, lambda *_: (0,) * len(shape), pipeline_mode=pl.Buffered(1))


def _emit_norms(hn, gain_refs, norm_refs, lo_ref):
    r = _rms(hn)
    for k, (g_ref, n_ref) in enumerate(zip(gain_refs, norm_refs)):
        xf = r * g_ref[...]
        hi = xf.astype(_BF16)
        n_ref[...] = hi
        if k == 0 and lo_ref is not None:
            lo_ref[...] = (xf - hi.astype(_F32)).astype(_BF16)


def _embed_kernel(x_ref, meta_ref, g_ref, h_ref, xn_ref):
    t = pl.program_id(1)
    tile = h_ref.shape[0]

    @pl.when(t == 0)
    def _():
        pad = tile - N_META
        meta = meta_ref[...]
        h_ref[0:pad, :] = jnp.zeros((pad, h_ref.shape[1]), _F32)
        h_ref[pad:, :] = meta
        xn_ref[0:pad, :] = jnp.zeros((pad, h_ref.shape[1]), _BF16)
        xn_ref[pad:, :] = (_rms(meta) * g_ref[...]).astype(_BF16)

    @pl.when(t > 0)
    def _():
        x = x_ref[0]
        h_ref[...] = x
        xn_ref[...] = (_rms(x) * g_ref[...]).astype(_BF16)


def _embed(x, meta_tokens, g, *, tile):
    batch, seq, d = x.shape
    nt = seq // tile + 1
    return pl.pallas_call(
        _embed_kernel,
        grid=(batch, nt),
        in_specs=[
            pl.BlockSpec((1, tile, d), lambda b, t: (b, jnp.maximum(t - 1, 0), 0)),
            pl.BlockSpec((N_META, d), lambda b, t: (0, 0)),
            pl.BlockSpec((1, d), lambda b, t: (0, 0)),
        ],
        out_specs=[
            pl.BlockSpec((tile, d), lambda b, t: (b * nt + t, 0)),
            pl.BlockSpec((tile, d), lambda b, t: (b * nt + t, 0)),
        ],
        out_shape=[
            jax.ShapeDtypeStruct((batch * nt * tile, d), _F32),
            jax.ShapeDtypeStruct((batch * nt * tile, d), _BF16),
        ],
        compiler_params=_params(2),
        name="embed",
    )(x, meta_tokens, g)


def _a_in_kernel(xn_ref, wv_ref, wg_ref, wz_ref, bv_ref, bg_ref, bz_ref, y_ref, zs_ref,
                 *, rows_per_batch, front_pad):
    i = pl.program_id(0)
    x = xn_ref[...]
    tm = x.shape[0]
    val = _dot(x, wv_ref[...]) + bv_ref[...]
    gate = _dot(x, wg_ref[...]) + bg_ref[...]
    z = _dot(x, wz_ref[...]) + bz_ref[...]
    pos = lax.rem(i * tm + lax.broadcasted_iota(jnp.int32, (tm, 1), 0), rows_per_batch)
    y = jnp.where(pos >= front_pad, val * jax.nn.sigmoid(gate), 0.0)
    y_ref[...] = y.astype(_BF16)
    zs_ref[...] = (z * jax.nn.sigmoid(z)).astype(_BF16)


def _a_in(xn, w_in, b_in, *, tile, col_tile, rows_per_batch, front_pad):
    rows, d = xn.shape
    e = w_in.shape[1] // 3
    nj = e // col_tile
    kern = functools.partial(_a_in_kernel, rows_per_batch=rows_per_batch, front_pad=front_pad)
    wspec = lambda k: pl.BlockSpec((d, col_tile), lambda i, j, k=k: (0, j + k * nj))
    bspec = lambda k: pl.BlockSpec((1, col_tile), lambda i, j, k=k: (0, j + k * nj))
    return pl.pallas_call(
        kern,
        grid=(rows // tile, nj),
        in_specs=[
            pl.BlockSpec((tile, d), lambda i, j: (i, 0)),
            wspec(0), wspec(1), wspec(2),
            bspec(0), bspec(1), bspec(2),
        ],
        out_specs=[
            pl.BlockSpec((tile, col_tile), lambda i, j: (i, j)),
            pl.BlockSpec((tile, col_tile), lambda i, j: (i, j)),
        ],
        out_shape=[
            jax.ShapeDtypeStruct((rows, e), _BF16),
            jax.ShapeDtypeStruct((rows, e), _BF16),
        ],
        compiler_params=_params(2),
        name="a_in",
    )(xn, w_in, w_in, w_in, b_in, b_in, b_in)


def _a_out_kernel(*refs, n_norms, emit_lo, conv_rows, norm_rows):
    (y_ref, halo_ref, zs_ref, h_ref, dww_ref, dwb_ref, lng_ref, lnb_ref, w_ref), refs = refs[:9], refs[9:]
    gain_refs, refs = refs[:n_norms], refs[n_norms:]
    o_ref, refs = refs[0], refs[1:]
    norm_refs, refs = refs[:n_norms], refs[n_norms:]
    lo_ref = None
    if emit_lo:
        lo_ref, refs = refs[0], refs[1:]
    shift_ref, conv_ref, u_ref = refs

    tm, e = y_ref.shape
    for c in range(e // 128):
        cs = pl.ds(c * 128, 128)
        strip = jnp.concatenate([halo_ref[:, cs], y_ref[:, cs]], axis=0).astype(_F32)
        shift_ref[0] = strip
        for b in range(1, 8):
            shift_ref[b] = pltpu.roll(strip, b, 0)

        def conv_body(rb, carry, cs=cs):
            r0 = pl.multiple_of(rb * conv_rows, conv_rows)
            acc = jnp.broadcast_to(dwb_ref[:, cs], (conv_rows, 128))
            for k in range(CONV_KERNEL):
                d = CONV_KERNEL - 1 - k
                src = shift_ref[d % 8, pl.ds(r0 + CONV_HALO - 8 * (d // 8), conv_rows), :]
                acc = acc + dww_ref[pl.ds(k, 1), cs] * src
            conv_ref[pl.ds(r0, conv_rows), cs] = acc
            return carry

        lax.fori_loop(0, tm // conv_rows, conv_body, 0)

    def norm_body(rb, carry):
        r0 = pl.multiple_of(rb * norm_rows, norm_rows)
        rs = pl.ds(r0, norm_rows)
        v = conv_ref[rs, :]
        mu = jnp.mean(v, axis=-1, keepdims=True)
        vc = v - mu
        var = jnp.mean(vc * vc, axis=-1, keepdims=True)
        yn = vc * lax.rsqrt(var + EPS) * lng_ref[...] + lnb_ref[...]
        u = yn * jax.nn.sigmoid(yn) * zs_ref[rs, :].astype(_F32)
        u_ref[rs, :] = u.astype(_BF16)
        return carry

    lax.fori_loop(0, tm // norm_rows, norm_body, 0)
    hn = h_ref[...] + _dot(u_ref[...], w_ref[...])
    o_ref[...] = hn
    _emit_norms(hn, gain_refs, norm_refs, lo_ref)


def _a_out(y, zs, h, dw_w, dw_b, ln_g, ln_b, w_out, gains, *, tile, emit_lo):
    rows, e = y.shape
    d = h.shape[1]
    halo_blocks = tile // CONV_HALO
    n_norms = len(gains)
    kern = functools.partial(_a_out_kernel, n_norms=n_norms, emit_lo=emit_lo,
                             conv_rows=min(A_CONV_ROWS, tile), norm_rows=min(A_NORM_ROWS, tile))
    row = lambda i: (i, 0)
    n_bf16_out = n_norms + (1 if emit_lo else 0)
    return pl.pallas_call(
        kern,
        grid=(rows // tile,),
        in_specs=[
            pl.BlockSpec((tile, e), row),
            pl.BlockSpec((CONV_HALO, e), lambda i: (jnp.maximum(i * halo_blocks - 1, 0), 0)),
            pl.BlockSpec((tile, e), row),
            pl.BlockSpec((tile, d), row),
            _resident((CONV_KERNEL, e)),
            _resident((1, e)),
            _resident((1, e)),
            _resident((1, e)),
            _resident((e, d)),
        ] + [_resident((1, d))] * n_norms,
        out_specs=[pl.BlockSpec((tile, d), row)] * (1 + n_bf16_out),
        out_shape=[jax.ShapeDtypeStruct((rows, d), _F32)]
        + [jax.ShapeDtypeStruct((rows, d), _BF16)] * n_bf16_out,
        scratch_shapes=[
            pltpu.VMEM((8, tile + CONV_HALO, 128), _F32),
            pltpu.VMEM((tile, e), _F32),
            pltpu.VMEM((tile, e), _BF16),
        ],
        compiler_params=_params(1),
        name="a_out",
    )(y, y, zs, h, dw_w, dw_b, ln_g, ln_b, w_out, *gains)


def _kv_kernel(xn_ref, xlo_ref, wk_ref, wvt_ref, wfh_ref, wfl_ref, bf_ref, sel_ref,
               kp_ref, vt_ref, ct_ref, ccol_ref, carry_ref,
               *, rows_per_batch, front_pad, heads_per_step):
    i = pl.program_id(0)
    tm = xn_ref.shape[0]
    nh = wfh_ref.shape[0]
    row0 = i * tm
    x = xn_ref[...]

    @pl.when(pl.program_id(1) == 0)
    def _():
        ft = (_dot_nt(wfh_ref[...], x) + _dot_nt(wfh_ref[...], xlo_ref[...])
              + _dot_nt(wfl_ref[...], x) + bf_ref[...])
        ls = jnp.minimum(ft, 0.0) - jnp.log1p(jnp.exp(-jnp.abs(ft)))
        pos = lax.rem(row0 + lax.broadcasted_iota(jnp.int32, (1, tm), 1), rows_per_batch)
        ls = jnp.where(pos >= front_pad, ls, 0.0)
        carry = jnp.where(lax.rem(row0, rows_per_batch) == 0, 0.0, carry_ref[...])
        tri = (lax.broadcasted_iota(jnp.int32, (tm, tm), 0)
               <= lax.broadcasted_iota(jnp.int32, (tm, tm), 1)).astype(_BF16)
        pieces = _dot(jnp.concatenate(_split3(ls), axis=0), tri)
        c = pieces[0:nh] + pieces[nh:2 * nh] + pieces[2 * nh:3 * nh] + carry
        carry_ref[...] = c[:, tm - 1:tm]
        hi, mid, lo = [p.astype(_F32) for p in _split3(c * LOG2E)]
        r = lax.broadcasted_iota(jnp.int32, (AUG_ROWS, tm), 0)
        ones_rows = jnp.where(r < 6, 1.0, 0.0)
        for hd in range(nh):
            blk = jnp.where(r == 0, hi[hd:hd + 1],
                            jnp.where(r == 1, mid[hd:hd + 1],
                                      jnp.where(r == 2, lo[hd:hd + 1], ones_rows)))
            ct_ref[hd] = blk.astype(_BF16)
        stack = jnp.concatenate([hi, mid, lo, jnp.zeros((128 - 3 * nh, tm), _F32)], axis=0)
        ccol_ref[...] = stack.T.astype(_BF16)

    kk = _dot(x, wk_ref[...])
    ka = _dot(ccol_ref[...], sel_ref[...])
    width = ka.shape[1]
    lane = lax.broadcasted_iota(jnp.int32, (tm, width), 1) & (HEAD_DIM - 1)
    pos = lax.rem(row0 + lax.broadcasted_iota(jnp.int32, (tm, 1), 0), rows_per_batch)
    ka = jnp.where(lane < 3, 1.0, ka)
    ka = jnp.where(jnp.logical_and(lane == 3, pos < front_pad), MASKED_KEY, ka)
    vt = _dot_nt(wvt_ref[...], x)
    ones_blk = jnp.where(lax.broadcasted_iota(jnp.int32, (BF16_SUBLANES, tm), 0) == 0, 1.0, 0.0).astype(_BF16)
    for hh in range(heads_per_step):
        hs = slice(hh * HEAD_DIM, (hh + 1) * HEAD_DIM)
        kp_ref[hh, :, 0:HEAD_DIM] = kk[:, hs].astype(_BF16)
        kp_ref[hh, :, HEAD_DIM:QK_DEPTH] = ka[:, hs].astype(_BF16)
        vt_ref[hh, 0, 0:HEAD_DIM, :] = vt[hs, :].astype(_BF16)
        vt_ref[hh, 0, HEAD_DIM:V_ROWS, :] = ones_blk


def _kv(xn, xlo, wk, wvt, wfh, wfl, bf, sel, *, tile, heads_per_step, rows_per_batch, front_pad):
    rows, d = xn.shape
    nh = wfh.shape[0]
    cw = heads_per_step * HEAD_DIM
    kern = functools.partial(_kv_kernel, rows_per_batch=rows_per_batch, front_pad=front_pad,
                             heads_per_step=heads_per_step)
    return pl.pallas_call(
        kern,
        grid=(rows // tile, nh // heads_per_step),
        in_specs=[
            pl.BlockSpec((tile, d), lambda i, j: (i, 0)),
            pl.BlockSpec((tile, d), lambda i, j: (i, 0)),
            pl.BlockSpec((d, cw), lambda i, j: (0, j)),
            pl.BlockSpec((cw, d), lambda i, j: (j, 0)),
            pl.BlockSpec((nh, d), lambda i, j: (0, 0)),
            pl.BlockSpec((nh, d), lambda i, j: (0, 0)),
            pl.BlockSpec((nh, 1), lambda i, j: (0, 0)),
            pl.BlockSpec((128, cw), lambda i, j: (0, j)),
        ],
        out_specs=[
            pl.BlockSpec((heads_per_step, tile, QK_DEPTH), lambda i, j: (j, i, 0)),
            pl.BlockSpec((heads_per_step, 1, V_ROWS, tile), lambda i, j: (j, i, 0, 0)),
            pl.BlockSpec((nh, AUG_ROWS, tile), lambda i, j: (0, 0, i)),
        ],
        out_shape=[
            jax.ShapeDtypeStruct((nh, rows, QK_DEPTH), _BF16),
            jax.ShapeDtypeStruct((nh, rows // tile, V_ROWS, tile), _BF16),
            jax.ShapeDtypeStruct((nh, AUG_ROWS, rows), _BF16),
        ],
        scratch_shapes=[
            pltpu.VMEM((tile, 128), _BF16),
            pltpu.VMEM((nh, 1), _F32),
        ],
        compiler_params=_params(2),
        name="kv_proj",
    )(xn, xlo, wk, wvt, wfh, wfl, bf, sel)


def _q_kernel(xn_ref, wqt_ref, wz_ref, qt_ref, zs_ref, *, heads_per_step, q_scale):
    x = xn_ref[...]
    qt = _dot_nt(wqt_ref[...], x) * q_scale
    for hh in range(heads_per_step):
        qt_ref[hh] = qt[hh * HEAD_DIM:(hh + 1) * HEAD_DIM, :].astype(_BF16)
    z = _dot(x, wz_ref[...])
    zs_ref[...] = (z * jax.nn.sigmoid(z)).astype(_BF16)


def _q_proj(xn, wqt, wz, *, tile, heads_per_step):
    rows, d = xn.shape
    a = wz.shape[1]
    nh = a // HEAD_DIM
    cw = heads_per_step * HEAD_DIM
    kern = functools.partial(_q_kernel, heads_per_step=heads_per_step,
                             q_scale=HEAD_DIM ** -0.5 * LOG2E)
    return pl.pallas_call(
        kern,
        grid=(rows // tile, nh // heads_per_step),
        in_specs=[
            pl.BlockSpec((tile, d), lambda i, j: (i, 0)),
            pl.BlockSpec((cw, d), lambda i, j: (j, 0)),
            pl.BlockSpec((d, cw), lambda i, j: (0, j)),
        ],
        out_specs=[
            pl.BlockSpec((heads_per_step, HEAD_DIM, tile), lambda i, j: (j, 0, i)),
            pl.BlockSpec((tile, cw), lambda i, j: (i, j)),
        ],
        out_shape=[
            jax.ShapeDtypeStruct((nh, HEAD_DIM, rows), _BF16),
            jax.ShapeDtypeStruct((rows, a), _BF16),
        ],
        compiler_params=_params(2),
        name="q_proj",
    )(xn, wqt, wz)


def _attn_kernel(qt_ref, ct_ref, k_ref, vt_ref, o_ref, acc_ref, m_ref, sa_ref, sb_ref, *, pairs_per_trip):
    qi = pl.program_id(2)
    tq = qt_ref.shape[2]
    tk = vt_ref.shape[3]
    qa = jnp.concatenate(
        [qt_ref[0], ct_ref[0], jnp.zeros((QK_DEPTH - HEAD_DIM - AUG_ROWS, tq), _BF16)], axis=0)
    m_ref[...] = jnp.full(m_ref.shape, -jnp.inf, _F32)
    acc_ref[...] = jnp.zeros(acc_ref.shape, _F32)

    def logits(j, s_ref):
        ks = k_ref[0, pl.ds(pl.multiple_of(j * tk, tk), tk), :]
        s_ref[...] = _dot(ks, qa)

    def consume(j, s_ref, on_diagonal):
        s = s_ref[...]
        if on_diagonal:
            key = lax.broadcasted_iota(jnp.int32, (tk, tq), 0)
            qry = lax.broadcasted_iota(jnp.int32, (tk, tq), 1)
            s = jnp.where(key <= qry, s, -jnp.inf)
        m_prev = m_ref[...]
        m_new = jnp.maximum(m_prev, jnp.max(s, axis=0, keepdims=True))
        p = jnp.exp2(s - m_new)
        alpha = jnp.exp2(m_prev - m_new)
        acc_ref[...] = alpha * acc_ref[...] + _dot(vt_ref[0, j], p.astype(_BF16))
        m_ref[...] = m_new

    logits(0, sa_ref)

    def pairs(j, n):
        for t in range(n):
            logits(j + 2 * t + 1, sb_ref)
            consume(j + 2 * t, sa_ref, False)
            logits(j + 2 * t + 2, sa_ref)
            consume(j + 2 * t + 1, sb_ref, False)

    trip = 2 * pairs_per_trip
    n_trips = qi // trip

    def trip_body(t, carry):
        pairs(t * trip, pairs_per_trip)
        return carry

    lax.fori_loop(0, n_trips, trip_body, 0)
    done = n_trips * trip
    n = pairs_per_trip // 2
    while n >= 1:
        @pl.when((qi & (2 * n)) != 0)
        def _(done=done, n=n):
            pairs(done, n)

        done = done + (qi & (2 * n))
        n //= 2

    @pl.when((qi & 1) != 0)
    def _():
        logits(qi, sb_ref)
        consume(qi - 1, sa_ref, False)
        consume(qi, sb_ref, True)

    @pl.when((qi & 1) == 0)
    def _():
        consume(qi, sa_ref, True)

    acc = acc_ref[...]
    o = acc[0:HEAD_DIM, :] / acc[HEAD_DIM:HEAD_DIM + 1, :]
    o_ref[...] = o.T.astype(_BF16)


def _attention(qt, ct, kp, vt, *, batch, tile):
    nh, _, rows = qt.shape
    nq = rows // tile // batch
    rows_per_batch = nq * tile
    return pl.pallas_call(
        functools.partial(_attn_kernel, pairs_per_trip=ATTN_PAIRS_PER_TRIP),
        grid=(nh, batch, nq),
        in_specs=[
            pl.BlockSpec((1, HEAD_DIM, tile), lambda h, b, q: (h, 0, b * nq + q)),
            pl.BlockSpec((1, AUG_ROWS, tile), lambda h, b, q: (h, 0, b * nq + q)),
            pl.BlockSpec((1, rows_per_batch, QK_DEPTH), lambda h, b, q: (h, b, 0)),
            pl.BlockSpec((1, nq, V_ROWS, tile), lambda h, b, q: (h, b, 0, 0)),
        ],
        out_specs=pl.BlockSpec((tile, HEAD_DIM), lambda h, b, q: (b * nq + q, h)),
        out_shape=jax.ShapeDtypeStruct((rows, nh * HEAD_DIM), _BF16),
        scratch_shapes=[
            pltpu.VMEM((V_ROWS, tile), _F32),
            pltpu.VMEM((1, tile), _F32),
            pltpu.VMEM((tile, tile), _F32),
            pltpu.VMEM((tile, tile), _F32),
        ],
        compiler_params=_params(3),
        name="fox_attention",
    )(qt, ct, kp, vt)


def _b_out_kernel(o_ref, zs_ref, h_ref, w_ref, g_ref, out_ref, xn_ref):
    hn = h_ref[...] + _dot(o_ref[...] * zs_ref[...], w_ref[...])
    out_ref[...] = hn
    _emit_norms(hn, [g_ref], [xn_ref], None)


def _b_out_final_kernel(o_ref, zs_ref, h_ref, w_ref, g_ref, out_ref):
    hn = h_ref[...] + _dot(o_ref[...] * zs_ref[...], w_ref[...])
    out_ref[...] = _rms(hn) * g_ref[...]


def _b_out(o, zs, h, w_out, g, *, tile):
    rows, a = o.shape
    d = h.shape[1]
    row = lambda i: (i, 0)
    return pl.pallas_call(
        _b_out_kernel,
        grid=(rows // tile,),
        in_specs=[
            pl.BlockSpec((tile, a), row),
            pl.BlockSpec((tile, a), row),
            pl.BlockSpec((tile, d), row),
            _resident((a, d)),
            _resident((1, d)),
        ],
        out_specs=[pl.BlockSpec((tile, d), row), pl.BlockSpec((tile, d), row)],
        out_shape=[jax.ShapeDtypeStruct((rows, d), _F32), jax.ShapeDtypeStruct((rows, d), _BF16)],
        compiler_params=_params(1),
        name="b_out",
    )(o, zs, h, w_out, g)


def _b_out_final(o, zs, h, w_out, g, *, tile, batch, seq):
    rows, a = o.shape
    d = h.shape[1]
    tiles_per_batch = rows // batch // tile
    out_tiles_per_batch = seq // tile
    row = lambda i: (i, 0)

    def out_map(i):
        b = i // tiles_per_batch
        t = i - b * tiles_per_batch
        return (b * out_tiles_per_batch + jnp.maximum(t - (tiles_per_batch - out_tiles_per_batch), 0), 0)

    return pl.pallas_call(
        _b_out_final_kernel,
        grid=(rows // tile,),
        in_specs=[
            pl.BlockSpec((tile, a), row),
            pl.BlockSpec((tile, a), row),
            pl.BlockSpec((tile, d), row),
            _resident((a, d)),
            _resident((1, d)),
        ],
        out_specs=pl.BlockSpec((tile, d), out_map),
        out_shape=jax.ShapeDtypeStruct((batch * seq, d), _F32),
        compiler_params=_params(1),
        name="b_out_final",
    )(o, zs, h, w_out, g)


def _forward(x, meta_tokens, a_norm_g, a_w_in, a_b_in, a_dw_w, a_dw_b, a_ln_g, a_ln_b, a_w_out,
             kv_norm_g, w_kvf, b_f, b_norm_g, b_w_in, b_w_out, final_norm_g,
             *, tile, a_in_rows, a_out_rows):
    batch, seq, d = x.shape
    n_a = a_w_in.shape[0]
    n_b = b_w_in.shape[0]
    attn_width = b_w_out.shape[1]
    nh = attn_width // HEAD_DIM
    heads_per_step = min(4, nh)
    assert seq % tile == 0 and N_META <= tile and a_out_rows % CONV_HALO == 0
    assert nh % heads_per_step == 0 and 3 * nh <= 128 and n_a >= 1 and n_b >= 1
    rows_per_batch = seq + tile
    front_pad = tile - N_META
    rows = batch * rows_per_batch
    assert rows % a_in_rows == 0 and rows % a_out_rows == 0
    col_tile = min(512, a_w_in.shape[2] // 3)

    row_vec = lambda v: v.reshape(1, -1).astype(_F32)
    x = x.astype(_F32)
    h, xn = _embed(x, meta_tokens.astype(_F32), row_vec(a_norm_g[0]), tile=tile)

    xn_kv = xlo_kv = None
    for layer in range(n_a):
        y, zs = _a_in(xn, a_w_in[layer].astype(_BF16), row_vec(a_b_in[layer]), tile=a_in_rows,
                      col_tile=col_tile, rows_per_batch=rows_per_batch, front_pad=front_pad)
        last = layer + 1 == n_a
        gains = [row_vec(kv_norm_g), row_vec(b_norm_g[0])] if last else [row_vec(a_norm_g[layer + 1])]
        outs = _a_out(y, zs, h, a_dw_w[layer].astype(_F32), row_vec(a_dw_b[layer]), row_vec(a_ln_g[layer]),
                      row_vec(a_ln_b[layer]), a_w_out[layer].astype(_BF16), gains,
                      tile=a_out_rows, emit_lo=last)
        if last:
            h, xn_kv, xn, xlo_kv = outs
        else:
            h, xn = outs

    wk = w_kvf[:, :attn_width].astype(_BF16)
    wvt = w_kvf[:, attn_width:2 * attn_width].T.astype(_BF16)
    wft = w_kvf[:, 2 * attn_width:].T.astype(_F32)
    wfh = wft.astype(_BF16)
    wfl = (wft - wfh.astype(_F32)).astype(_BF16)
    piece = jnp.arange(128)[:, None]
    col = jnp.arange(attn_width)[None, :]
    sel = jnp.where((piece < 3 * nh) & (col // HEAD_DIM == piece % nh)
                    & (col % HEAD_DIM == 3 + piece // nh), -1.0, 0.0).astype(_BF16)
    kp, vt, ct = _kv(xn_kv, xlo_kv, wk, wvt, wfh, wfl, b_f.reshape(nh, 1).astype(_F32), sel,
                     tile=tile, heads_per_step=heads_per_step,
                     rows_per_batch=rows_per_batch, front_pad=front_pad)

    for layer in range(n_b):
        wqt = b_w_in[layer][:, :attn_width].T.astype(_BF16)
        wz = b_w_in[layer][:, attn_width:].astype(_BF16)
        qt, zs = _q_proj(xn, wqt, wz, tile=a_in_rows, heads_per_step=heads_per_step)
        o = _attention(qt, ct, kp, vt, batch=batch, tile=tile)
        w_out = b_w_out[layer].astype(_BF16)
        if layer + 1 < n_b:
            h, xn = _b_out(o, zs, h, w_out, row_vec(b_norm_g[layer + 1]), tile=tile)
        else:
            h = _b_out_final(o, zs, h, w_out, row_vec(final_norm_g), tile=tile, batch=batch, seq=seq)
    return h.reshape(batch, seq, d)


def kernel(x, meta_tokens, a_norm_g, a_w_in, a_b_in, a_dw_w, a_dw_b, a_ln_g, a_ln_b, a_w_out,
           kv_norm_g, w_kvf, b_f, b_norm_g, b_w_in, b_w_out, final_norm_g):
    return _forward(x, meta_tokens, a_norm_g, a_w_in, a_b_in, a_dw_w, a_dw_b, a_ln_g, a_ln_b, a_w_out,
                    kv_norm_g, w_kvf, b_f, b_norm_g, b_w_in, b_w_out, final_norm_g,
                    tile=SEQ_TILE, a_in_rows=A_IN_ROWS, a_out_rows=A_OUT_ROWS)
```

```python
import functools

import jax
import jax.numpy as jnp
from jax import lax
from jax.experimental import pallas as pl
from jax.experimental.pallas import tpu as pltpu

N_META = 16
HEAD_DIM = 128
CONV_KERNEL = 31
EPS = 1e-6

SEQ_TILE = 512
A_IN_ROWS = 1024
A_OUT_ROWS = 256
A_CONV_ROWS = 128
A_NORM_ROWS = 256
ATTN_PAIRS_PER_TRIP = 4
CONV_HALO = 32
BF16_SUBLANES = 16
AUG_ROWS = BF16_SUBLANES
V_ROWS = HEAD_DIM + BF16_SUBLANES
QK_DEPTH = 2 * HEAD_DIM
LOG2E = 1.4426950408889634
MASKED_KEY = -1e30
VMEM_LIMIT_BYTES = 56 * 1024 * 1024

_F32 = jnp.float32
_BF16 = jnp.bfloat16


def _dot(a, b):
    return jnp.dot(a, b, preferred_element_type=_F32)


def _dot_nt(a, b):
    return lax.dot_general(a, b, (((1,), (1,)), ((), ())), preferred_element_type=_F32)


def _rms(x):
    ms = jnp.mean(x * x, axis=-1, keepdims=True)
    return x * lax.rsqrt(ms + EPS)


def _split3(x):
    hi = x.astype(_BF16)
    r = x - hi.astype(_F32)
    mid = r.astype(_BF16)
    lo = (r - mid.astype(_F32)).astype(_BF16)
    return hi, mid, lo


def _params(n_grid_dims):
    return pltpu.CompilerParams(
        dimension_semantics=("arbitrary",) * n_grid_dims,
        vmem_limit_bytes=VMEM_LIMIT_BYTES,
    )


def _resident(shape):
    return pl.BlockSpec(shape, lambda *_: (0,) * len(shape), pipeline_mode=pl.Buffered(1))


def _emit_norms(hn, gain_refs, norm_refs, lo_ref):
    r = _rms(hn)
    for k, (g_ref, n_ref) in enumerate(zip(gain_refs, norm_refs)):
        xf = r * g_ref[...]
        hi = xf.astype(_BF16)
        n_ref[...] = hi
        if k == 0 and lo_ref is not None:
            lo_ref[...] = (xf - hi.astype(_F32)).astype(_BF16)


def _embed_kernel(x_ref, meta_ref, g_ref, h_ref, xn_ref):
    t = pl.program_id(1)
    tile = h_ref.shape[0]

    @pl.when(t == 0)
    def _():
        pad = tile - N_META
        meta = meta_ref[...]
        h_ref[0:pad, :] = jnp.zeros((pad, h_ref.shape[1]), _F32)
        h_ref[pad:, :] = meta
        xn_ref[0:pad, :] = jnp.zeros((pad, h_ref.shape[1]), _BF16)
        xn_ref[pad:, :] = (_rms(meta) * g_ref[...]).astype(_BF16)

    @pl.when(t > 0)
    def _():
        x = x_ref[0]
        h_ref[...] = x
        xn_ref[...] = (_rms(x) * g_ref[...]).astype(_BF16)


def _embed(x, meta_tokens, g, *, tile):
    batch, seq, d = x.shape
    nt = seq // tile + 1
    return pl.pallas_call(
        _embed_kernel,
        grid=(batch, nt),
        in_specs=[
            pl.BlockSpec((1, tile, d), lambda b, t: (b, jnp.maximum(t - 1, 0), 0)),
            pl.BlockSpec((N_META, d), lambda b, t: (0, 0)),
            pl.BlockSpec((1, d), lambda b, t: (0, 0)),
        ],
        out_specs=[
            pl.BlockSpec((tile, d), lambda b, t: (b * nt + t, 0)),
            pl.BlockSpec((tile, d), lambda b, t: (b * nt + t, 0)),
        ],
        out_shape=[
            jax.ShapeDtypeStruct((batch * nt * tile, d), _F32),
            jax.ShapeDtypeStruct((batch * nt * tile, d), _BF16),
        ],
        compiler_params=_params(2),
        name="embed",
    )(x, meta_tokens, g)


def _a_in_kernel(xn_ref, wv_ref, wg_ref, wz_ref, bv_ref, bg_ref, bz_ref, y_ref, zs_ref,
                 *, rows_per_batch, front_pad):
    i = pl.program_id(0)
    x = xn_ref[...]
    tm = x.shape[0]
    val = _dot(x, wv_ref[...]) + bv_ref[...]
    gate = _dot(x, wg_ref[...]) + bg_ref[...]
    z = _dot(x, wz_ref[...]) + bz_ref[...]
    pos = lax.rem(i * tm + lax.broadcasted_iota(jnp.int32, (tm, 1), 0), rows_per_batch)
    y = jnp.where(pos >= front_pad, val * jax.nn.sigmoid(gate), 0.0)
    y_ref[...] = y.astype(_BF16)
    zs_ref[...] = (z * jax.nn.sigmoid(z)).astype(_BF16)


def _a_in(xn, w_in, b_in, *, tile, col_tile, rows_per_batch, front_pad):
    rows, d = xn.shape
    e = w_in.shape[1] // 3
    nj = e // col_tile
    kern = functools.partial(_a_in_kernel, rows_per_batch=rows_per_batch, front_pad=front_pad)
    wspec = lambda k: pl.BlockSpec((d, col_tile), lambda i, j, k=k: (0, j + k * nj))
    bspec = lambda k: pl.BlockSpec((1, col_tile), lambda i, j, k=k: (0, j + k * nj))
    return pl.pallas_call(
        kern,
        grid=(rows // tile, nj),
        in_specs=[
            pl.BlockSpec((tile, d), lambda i, j: (i, 0)),
            wspec(0), wspec(1), wspec(2),
            bspec(0), bspec(1), bspec(2),
        ],
        out_specs=[
            pl.BlockSpec((tile, col_tile), lambda i, j: (i, j)),
            pl.BlockSpec((tile, col_tile), lambda i, j: (i, j)),
        ],
        out_shape=[
            jax.ShapeDtypeStruct((rows, e), _BF16),
            jax.ShapeDtypeStruct((rows, e), _BF16),
        ],
        compiler_params=_params(2),
        name="a_in",
    )(xn, w_in, w_in, w_in, b_in, b_in, b_in)


def _a_out_kernel(*refs, n_norms, emit_lo, conv_rows, norm_rows):
    (y_ref, halo_ref, zs_ref, h_ref, dww_ref, dwb_ref, lng_ref, lnb_ref, w_ref), refs = refs[:9], refs[9:]
    gain_refs, refs = refs[:n_norms], refs[n_norms:]
    o_ref, refs = refs[0], refs[1:]
    norm_refs, refs = refs[:n_norms], refs[n_norms:]
    lo_ref = None
    if emit_lo:
        lo_ref, refs = refs[0], refs[1:]
    shift_ref, conv_ref, u_ref = refs

    tm, e = y_ref.shape
    for c in range(e // 128):
        cs = pl.ds(c * 128, 128)
        strip = jnp.concatenate([halo_ref[:, cs], y_ref[:, cs]], axis=0).astype(_F32)
        shift_ref[0] = strip
        for b in range(1, 8):
            shift_ref[b] = pltpu.roll(strip, b, 0)

        def conv_body(rb, carry, cs=cs):
            r0 = pl.multiple_of(rb * conv_rows, conv_rows)
            acc = jnp.broadcast_to(dwb_ref[:, cs], (conv_rows, 128))
            for k in range(CONV_KERNEL):
                d = CONV_KERNEL - 1 - k
                src = shift_ref[d % 8, pl.ds(r0 + CONV_HALO - 8 * (d // 8), conv_rows), :]
                acc = acc + dww_ref[pl.ds(k, 1), cs] * src
            conv_ref[pl.ds(r0, conv_rows), cs] = acc
            return carry

        lax.fori_loop(0, tm // conv_rows, conv_body, 0)

    def norm_body(rb, carry):
        r0 = pl.multiple_of(rb * norm_rows, norm_rows)
        rs = pl.ds(r0, norm_rows)
        v = conv_ref[rs, :]
        mu = jnp.mean(v, axis=-1, keepdims=True)
        vc = v - mu
        var = jnp.mean(vc * vc, axis=-1, keepdims=True)
        yn = vc * lax.rsqrt(var + EPS) * lng_ref[...] + lnb_ref[...]
        u = yn * jax.nn.sigmoid(yn) * zs_ref[rs, :].astype(_F32)
        u_ref[rs, :] = u.astype(_BF16)
        return carry

    lax.fori_loop(0, tm // norm_rows, norm_body, 0)
    hn = h_ref[...] + _dot(u_ref[...], w_ref[...])
    o_ref[...] = hn
    _emit_norms(hn, gain_refs, norm_refs, lo_ref)


def _a_out(y, zs, h, dw_w, dw_b, ln_g, ln_b, w_out, gains, *, tile, emit_lo):
    rows, e = y.shape
    d = h.shape[1]
    halo_blocks = tile // CONV_HALO
    n_norms = len(gains)
    kern = functools.partial(_a_out_kernel, n_norms=n_norms, emit_lo=emit_lo,
                             conv_rows=min(A_CONV_ROWS, tile), norm_rows=min(A_NORM_ROWS, tile))
    row = lambda i: (i, 0)
    n_bf16_out = n_norms + (1 if emit_lo else 0)
    return pl.pallas_call(
        kern,
        grid=(rows // tile,),
        in_specs=[
            pl.BlockSpec((tile, e), row),
            pl.BlockSpec((CONV_HALO, e), lambda i: (jnp.maximum(i * halo_blocks - 1, 0), 0)),
            pl.BlockSpec((tile, e), row),
            pl.BlockSpec((tile, d), row),
            _resident((CONV_KERNEL, e)),
            _resident((1, e)),
            _resident((1, e)),
            _resident((1, e)),
            _resident((e, d)),
        ] + [_resident((1, d))] * n_norms,
        out_specs=[pl.BlockSpec((tile, d), row)] * (1 + n_bf16_out),
        out_shape=[jax.ShapeDtypeStruct((rows, d), _F32)]
        + [jax.ShapeDtypeStruct((rows, d), _BF16)] * n_bf16_out,
        scratch_shapes=[
            pltpu.VMEM((8, tile + CONV_HALO, 128), _F32),
            pltpu.VMEM((tile, e), _F32),
            pltpu.VMEM((tile, e), _BF16),
        ],
        compiler_params=_params(1),
        name="a_out",
    )(y, y, zs, h, dw_w, dw_b, ln_g, ln_b, w_out, *gains)


def _kv_kernel(xn_ref, xlo_ref, wk_ref, wvt_ref, wfh_ref, wfl_ref, bf_ref, sel_ref,
               kp_ref, vt_ref, ct_ref, ccol_ref, carry_ref,
               *, rows_per_batch, front_pad, heads_per_step):
    i = pl.program_id(0)
    tm = xn_ref.shape[0]
    nh = wfh_ref.shape[0]
    row0 = i * tm
    x = xn_ref[...]

    @pl.when(pl.program_id(1) == 0)
    def _():
        ft = (_dot_nt(wfh_ref[...], x) + _dot_nt(wfh_ref[...], xlo_ref[...])
              + _dot_nt(wfl_ref[...], x) + bf_ref[...])
        ls = jnp.minimum(ft, 0.0) - jnp.log1p(jnp.exp(-jnp.abs(ft)))
        pos = lax.rem(row0 + lax.broadcasted_iota(jnp.int32, (1, tm), 1), rows_per_batch)
        ls = jnp.where(pos >= front_pad, ls, 0.0)
        carry = jnp.where(lax.rem(row0, rows_per_batch) == 0, 0.0, carry_ref[...])
        tri = (lax.broadcasted_iota(jnp.int32, (tm, tm), 0)
               <= lax.broadcasted_iota(jnp.int32, (tm, tm), 1)).astype(_BF16)
        pieces = _dot(jnp.concatenate(_split3(ls), axis=0), tri)
        c = pieces[0:nh] + pieces[nh:2 * nh] + pieces[2 * nh:3 * nh] + carry
        carry_ref[...] = c[:, tm - 1:tm]
        hi, mid, lo = [p.astype(_F32) for p in _split3(c * LOG2E)]
        r = lax.broadcasted_iota(jnp.int32, (AUG_ROWS, tm), 0)
        ones_rows = jnp.where(r < 6, 1.0, 0.0)
        for hd in range(nh):
            blk = jnp.where(r == 0, hi[hd:hd + 1],
                            jnp.where(r == 1, mid[hd:hd + 1],
                                      jnp.where(r == 2, lo[hd:hd + 1], ones_rows)))
            ct_ref[hd] = blk.astype(_BF16)
        stack = jnp.concatenate([hi, mid, lo, jnp.zeros((128 - 3 * nh, tm), _F32)], axis=0)
        ccol_ref[...] = stack.T.astype(_BF16)

    kk = _dot(x, wk_ref[...])
    ka = _dot(ccol_ref[...], sel_ref[...])
    width = ka.shape[1]
    lane = lax.broadcasted_iota(jnp.int32, (tm, width), 1) & (HEAD_DIM - 1)
    pos = lax.rem(row0 + lax.broadcasted_iota(jnp.int32, (tm, 1), 0), rows_per_batch)
    ka = jnp.where(lane < 3, 1.0, ka)
    ka = jnp.where(jnp.logical_and(lane == 3, pos < front_pad), MASKED_KEY, ka)
    vt = _dot_nt(wvt_ref[...], x)
    ones_blk = jnp.where(lax.broadcasted_iota(jnp.int32, (BF16_SUBLANES, tm), 0) == 0, 1.0, 0.0).astype(_BF16)
    for hh in range(heads_per_step):
        hs = slice(hh * HEAD_DIM, (hh + 1) * HEAD_DIM)
        kp_ref[hh, :, 0:HEAD_DIM] = kk[:, hs].astype(_BF16)
        kp_ref[hh, :, HEAD_DIM:QK_DEPTH] = ka[:, hs].astype(_BF16)
        vt_ref[hh, 0, 0:HEAD_DIM, :] = vt[hs, :].astype(_BF16)
        vt_ref[hh, 0, HEAD_DIM:V_ROWS, :] = ones_blk


def _kv(xn, xlo, wk, wvt, wfh, wfl, bf, sel, *, tile, heads_per_step, rows_per_batch, front_pad):
    rows, d = xn.shape
    nh = wfh.shape[0]
    cw = heads_per_step * HEAD_DIM
    kern = functools.partial(_kv_kernel, rows_per_batch=rows_per_batch, front_pad=front_pad,
                             heads_per_step=heads_per_step)
    return pl.pallas_call(
        kern,
        grid=(rows // tile, nh // heads_per_step),
        in_specs=[
            pl.BlockSpec((tile, d), lambda i, j: (i, 0)),
            pl.BlockSpec((tile, d), lambda i, j: (i, 0)),
            pl.BlockSpec((d, cw), lambda i, j: (0, j)),
            pl.BlockSpec((cw, d), lambda i, j: (j, 0)),
            pl.BlockSpec((nh, d), lambda i, j: (0, 0)),
            pl.BlockSpec((nh, d), lambda i, j: (0, 0)),
            pl.BlockSpec((nh, 1), lambda i, j: (0, 0)),
            pl.BlockSpec((128, cw), lambda i, j: (0, j)),
        ],
        out_specs=[
            pl.BlockSpec((heads_per_step, tile, QK_DEPTH), lambda i, j: (j, i, 0)),
            pl.BlockSpec((heads_per_step, 1, V_ROWS, tile), lambda i, j: (j, i, 0, 0)),
            pl.BlockSpec((nh, AUG_ROWS, tile), lambda i, j: (0, 0, i)),
        ],
        out_shape=[
            jax.ShapeDtypeStruct((nh, rows, QK_DEPTH), _BF16),
            jax.ShapeDtypeStruct((nh, rows // tile, V_ROWS, tile), _BF16),
            jax.ShapeDtypeStruct((nh, AUG_ROWS, rows), _BF16),
        ],
        scratch_shapes=[
            pltpu.VMEM((tile, 128), _BF16),
            pltpu.VMEM((nh, 1), _F32),
        ],
        compiler_params=_params(2),
        name="kv_proj",
    )(xn, xlo, wk, wvt, wfh, wfl, bf, sel)


def _q_kernel(xn_ref, wqt_ref, wz_ref, qt_ref, zs_ref, *, heads_per_step, q_scale):
    x = xn_ref[...]
    qt = _dot_nt(wqt_ref[...], x) * q_scale
    for hh in range(heads_per_step):
        qt_ref[hh] = qt[hh * HEAD_DIM:(hh + 1) * HEAD_DIM, :].astype(_BF16)
    z = _dot(x, wz_ref[...])
    zs_ref[...] = (z * jax.nn.sigmoid(z)).astype(_BF16)


def _q_proj(xn, wqt, wz, *, tile, heads_per_step):
    rows, d = xn.shape
    a = wz.shape[1]
    nh = a // HEAD_DIM
    cw = heads_per_step * HEAD_DIM
    kern = functools.partial(_q_kernel, heads_per_step=heads_per_step,
                             q_scale=HEAD_DIM ** -0.5 * LOG2E)
    return pl.pallas_call(
        kern,
        grid=(rows // tile, nh // heads_per_step),
        in_specs=[
            pl.BlockSpec((tile, d), lambda i, j: (i, 0)),
            pl.BlockSpec((cw, d), lambda i, j: (j, 0)),
            pl.BlockSpec((d, cw), lambda i, j: (0, j)),
        ],
        out_specs=[
            pl.BlockSpec((heads_per_step, HEAD_DIM, tile), lambda i, j: (j, 0, i)),
            pl.BlockSpec((tile, cw), lambda i, j: (i, j)),
        ],
        out_shape=[
            jax.ShapeDtypeStruct((nh, HEAD_DIM, rows), _BF16),
            jax.ShapeDtypeStruct((rows, a), _BF16),
        ],
        compiler_params=_params(2),
        name="q_proj",
    )(xn, wqt, wz)


def _attn_kernel(qt_ref, ct_ref, k_ref, vt_ref, o_ref, acc_ref, m_ref, sa_ref, sb_ref, *, pairs_per_trip):
    nq = vt_ref.shape[1]
    tq = tk = vt_ref.shape[3]

    def q_aug(qi):
        cols = pl.ds(pl.multiple_of(qi * tq, tq), tq)
        return jnp.concatenate(
            [qt_ref[0, :, cols], ct_ref[0, :, cols],
             jnp.zeros((QK_DEPTH - HEAD_DIM - AUG_ROWS, tq), _BF16)], axis=0)

    def logits(qa, j, s_ref):
        ks = k_ref[0, pl.ds(pl.multiple_of(j * tk, tk), tk), :]
        s_ref[...] = _dot(ks, qa)

    def consume(j, s_ref, on_diagonal):
        s = s_ref[...]
        if on_diagonal:
            key = lax.broadcasted_iota(jnp.int32, (tk, tq), 0)
            qry = lax.broadcasted_iota(jnp.int32, (tk, tq), 1)
            s = jnp.where(key <= qry, s, -jnp.inf)
        m_prev = m_ref[...]
        m_new = jnp.maximum(m_prev, jnp.max(s, axis=0, keepdims=True))
        p = jnp.exp2(s - m_new)
        alpha = jnp.exp2(m_prev - m_new)
        acc_ref[...] = alpha * acc_ref[...] + _dot(vt_ref[0, j], p.astype(_BF16))
        m_ref[...] = m_new

    logits(q_aug(0), 0, sa_ref)
    trip = 2 * pairs_per_trip

    def q_block(qi, carry):
        qa = q_aug(qi)
        qa_next = q_aug(jnp.minimum(qi + 1, nq - 1))
        m_ref[...] = jnp.full(m_ref.shape, -jnp.inf, _F32)
        acc_ref[...] = jnp.zeros(acc_ref.shape, _F32)

        def pairs(j, n):
            for t in range(n):
                logits(qa, j + 2 * t + 1, sb_ref)
                consume(j + 2 * t, sa_ref, False)
                logits(qa, j + 2 * t + 2, sa_ref)
                consume(j + 2 * t + 1, sb_ref, False)

        n_trips = qi // trip

        def trip_body(t, c):
            pairs(t * trip, pairs_per_trip)
            return c

        lax.fori_loop(0, n_trips, trip_body, 0)
        done = n_trips * trip
        n = pairs_per_trip // 2
        while n >= 1:
            @pl.when((qi & (2 * n)) != 0)
            def _(done=done, n=n):
                pairs(done, n)

            done = done + (qi & (2 * n))
            n //= 2

        @pl.when((qi & 1) != 0)
        def _():
            logits(qa, qi, sb_ref)
            consume(qi - 1, sa_ref, False)
            logits(qa_next, 0, sa_ref)
            consume(qi, sb_ref, True)

        @pl.when((qi & 1) == 0)
        def _():
            consume(qi, sa_ref, True)
            logits(qa_next, 0, sa_ref)

        acc = acc_ref[...]
        o = acc[0:HEAD_DIM, :] / acc[HEAD_DIM:HEAD_DIM + 1, :]
        o_ref[pl.ds(pl.multiple_of(qi * tq, tq), tq), :] = o.T.astype(_BF16)
        return carry

    lax.fori_loop(0, nq, q_block, 0)


def _attention(qt, ct, kp, vt, *, batch, tile):
    nh, _, rows = qt.shape
    nq = rows // tile // batch
    rows_per_batch = nq * tile
    return pl.pallas_call(
        functools.partial(_attn_kernel, pairs_per_trip=ATTN_PAIRS_PER_TRIP),
        grid=(nh, batch),
        in_specs=[
            pl.BlockSpec((1, HEAD_DIM, rows_per_batch), lambda h, b: (h, 0, b)),
            pl.BlockSpec((1, AUG_ROWS, rows_per_batch), lambda h, b: (h, 0, b)),
            pl.BlockSpec((1, rows_per_batch, QK_DEPTH), lambda h, b: (h, b, 0)),
            pl.BlockSpec((1, nq, V_ROWS, tile), lambda h, b: (h, b, 0, 0)),
        ],
        out_specs=pl.BlockSpec((rows_per_batch, HEAD_DIM), lambda h, b: (b, h)),
        out_shape=jax.ShapeDtypeStruct((rows, nh * HEAD_DIM), _BF16),
        scratch_shapes=[
            pltpu.VMEM((V_ROWS, tile), _F32),
            pltpu.VMEM((1, tile), _F32),
            pltpu.VMEM((tile, tile), _F32),
            pltpu.VMEM((tile, tile), _F32),
        ],
        compiler_params=_params(2),
        name="fox_attention",
    )(qt, ct, kp, vt)


def _b_out_kernel(o_ref, zs_ref, h_ref, w_ref, g_ref, out_ref, xn_ref):
    hn = h_ref[...] + _dot(o_ref[...] * zs_ref[...], w_ref[...])
    out_ref[...] = hn
    _emit_norms(hn, [g_ref], [xn_ref], None)


def _b_out_final_kernel(o_ref, zs_ref, h_ref, w_ref, g_ref, out_ref):
    hn = h_ref[...] + _dot(o_ref[...] * zs_ref[...], w_ref[...])
    out_ref[...] = _rms(hn) * g_ref[...]


def _b_out(o, zs, h, w_out, g, *, tile):
    rows, a = o.shape
    d = h.shape[1]
    row = lambda i: (i, 0)
    return pl.pallas_call(
        _b_out_kernel,
        grid=(rows // tile,),
        in_specs=[
            pl.BlockSpec((tile, a), row),
            pl.BlockSpec((tile, a), row),
            pl.BlockSpec((tile, d), row),
            _resident((a, d)),
            _resident((1, d)),
        ],
        out_specs=[pl.BlockSpec((tile, d), row), pl.BlockSpec((tile, d), row)],
        out_shape=[jax.ShapeDtypeStruct((rows, d), _F32), jax.ShapeDtypeStruct((rows, d), _BF16)],
        compiler_params=_params(1),
        name="b_out",
    )(o, zs, h, w_out, g)


def _b_out_final(o, zs, h, w_out, g, *, tile, batch, seq):
    rows, a = o.shape
    d = h.shape[1]
    tiles_per_batch = rows // batch // tile
    out_tiles_per_batch = seq // tile
    row = lambda i: (i, 0)

    def out_map(i):
        b = i // tiles_per_batch
        t = i - b * tiles_per_batch
        return (b * out_tiles_per_batch + jnp.maximum(t - (tiles_per_batch - out_tiles_per_batch), 0), 0)

    return pl.pallas_call(
        _b_out_final_kernel,
        grid=(rows // tile,),
        in_specs=[
            pl.BlockSpec((tile, a), row),
            pl.BlockSpec((tile, a), row),
            pl.BlockSpec((tile, d), row),
            _resident((a, d)),
            _resident((1, d)),
        ],
        out_specs=pl.BlockSpec((tile, d), out_map),
        out_shape=jax.ShapeDtypeStruct((batch * seq, d), _F32),
        compiler_params=_params(1),
        name="b_out_final",
    )(o, zs, h, w_out, g)


def _forward(x, meta_tokens, a_norm_g, a_w_in, a_b_in, a_dw_w, a_dw_b, a_ln_g, a_ln_b, a_w_out,
             kv_norm_g, w_kvf, b_f, b_norm_g, b_w_in, b_w_out, final_norm_g,
             *, tile, a_in_rows, a_out_rows):
    batch, seq, d = x.shape
    n_a = a_w_in.shape[0]
    n_b = b_w_in.shape[0]
    attn_width = b_w_out.shape[1]
    nh = attn_width // HEAD_DIM
    heads_per_step = min(4, nh)
    assert seq % tile == 0 and N_META <= tile and a_out_rows % CONV_HALO == 0
    assert nh % heads_per_step == 0 and 3 * nh <= 128 and n_a >= 1 and n_b >= 1
    rows_per_batch = seq + tile
    front_pad = tile - N_META
    rows = batch * rows_per_batch
    assert rows % a_in_rows == 0 and rows % a_out_rows == 0
    col_tile = min(512, a_w_in.shape[2] // 3)

    row_vec = lambda v: v.reshape(1, -1).astype(_F32)
    x = x.astype(_F32)
    h, xn = _embed(x, meta_tokens.astype(_F32), row_vec(a_norm_g[0]), tile=tile)

    xn_kv = xlo_kv = None
    for layer in range(n_a):
        y, zs = _a_in(xn, a_w_in[layer].astype(_BF16), row_vec(a_b_in[layer]), tile=a_in_rows,
                      col_tile=col_tile, rows_per_batch=rows_per_batch, front_pad=front_pad)
        last = layer + 1 == n_a
        gains = [row_vec(kv_norm_g), row_vec(b_norm_g[0])] if last else [row_vec(a_norm_g[layer + 1])]
        outs = _a_out(y, zs, h, a_dw_w[layer].astype(_F32), row_vec(a_dw_b[layer]), row_vec(a_ln_g[layer]),
                      row_vec(a_ln_b[layer]), a_w_out[layer].astype(_BF16), gains,
                      tile=a_out_rows, emit_lo=last)
        if last:
            h, xn_kv, xn, xlo_kv = outs
        else:
            h, xn = outs

    wk = w_kvf[:, :attn_width].astype(_BF16)
    wvt = w_kvf[:, attn_width:2 * attn_width].T.astype(_BF16)
    wft = w_kvf[:, 2 * attn_width:].T.astype(_F32)
    wfh = wft.astype(_BF16)
    wfl = (wft - wfh.astype(_F32)).astype(_BF16)
    piece = jnp.arange(128)[:, None]
    col = jnp.arange(attn_width)[None, :]
    sel = jnp.where((piece < 3 * nh) & (col // HEAD_DIM == piece % nh)
                    & (col % HEAD_DIM == 3 + piece // nh), -1.0, 0.0).astype(_BF16)
    kp, vt, ct = _kv(xn_kv, xlo_kv, wk, wvt, wfh, wfl, b_f.reshape(nh, 1).astype(_F32), sel,
                     tile=tile, heads_per_step=heads_per_step,
                     rows_per_batch=rows_per_batch, front_pad=front_pad)

    for layer in range(n_b):
        wqt = b_w_in[layer][:, :attn_width].T.astype(_BF16)
        wz = b_w_in[layer][:, attn_width:].astype(_BF16)
        qt, zs = _q_proj(xn, wqt, wz, tile=a_in_rows, heads_per_step=heads_per_step)
        o = _attention(qt, ct, kp, vt, batch=batch, tile=tile)
        w_out = b_w_out[layer].astype(_BF16)
        if layer + 1 < n_b:
            h, xn = _b_out(o, zs, h, w_out, row_vec(b_norm_g[layer + 1]), tile=tile)
        else:
            h = _b_out_final(o, zs, h, w_out, row_vec(final_norm_g), tile=tile, batch=batch, seq=seq)
    return h.reshape(batch, seq, d)


def kernel(x, meta_tokens, a_norm_g, a_w_in, a_b_in, a_dw_w, a_dw_b, a_ln_g, a_ln_b, a_w_out,
           kv_norm_g, w_kvf, b_f, b_norm_g, b_w_in, b_w_out, final_norm_g):
    return _forward(x, meta_tokens, a_norm_g, a_w_in, a_b_in, a_dw_w, a_dw_b, a_ln_g, a_ln_b, a_w_out,
                    kv_norm_g, w_kvf, b_f, b_norm_g, b_w_in, b_w_out, final_norm_g,
                    tile=SEQ_TILE, a_in_rows=A_IN_ROWS, a_out_rows=A_OUT_ROWS)
```

```python
import functools

import jax
import jax.numpy as jnp
from jax import lax
from jax.experimental import pallas as pl
from jax.experimental.pallas import tpu as pltpu

N_META = 16
HEAD_DIM = 128
CONV_KERNEL = 31
EPS = 1e-6

SEQ_TILE = 512
A_IN_ROWS = 1024
A_OUT_ROWS = 256
A_CONV_ROWS = 128
A_NORM_ROWS = 256
ATTN_PAIRS_PER_TRIP = 8
CONV_HALO = 32
BF16_SUBLANES = 16
AUG_ROWS = BF16_SUBLANES
V_ROWS = HEAD_DIM + BF16_SUBLANES
QK_DEPTH = 2 * HEAD_DIM
LOG2E = 1.4426950408889634
MASKED_KEY = -1e30
VMEM_LIMIT_BYTES = 56 * 1024 * 1024

_F32 = jnp.float32
_BF16 = jnp.bfloat16


def _dot(a, b):
    return jnp.dot(a, b, preferred_element_type=_F32)


def _dot_nt(a, b):
    return lax.dot_general(a, b, (((1,), (1,)), ((), ())), preferred_element_type=_F32)


def _rms(x):
    ms = jnp.mean(x * x, axis=-1, keepdims=True)
    return x * lax.rsqrt(ms + EPS)


def _split3(x):
    hi = x.astype(_BF16)
    r = x - hi.astype(_F32)
    mid = r.astype(_BF16)
    lo = (r - mid.astype(_F32)).astype(_BF16)
    return hi, mid, lo


def _params(n_grid_dims):
    return pltpu.CompilerParams(
        dimension_semantics=("arbitrary",) * n_grid_dims,
        vmem_limit_bytes=VMEM_LIMIT_BYTES,
    )


def _resident(shape):
    return pl.BlockSpec(shape, lambda *_: (0,) * len(shape), pipeline_mode=pl.Buffered(1))


def _emit_norms(hn, gain_refs, norm_refs, lo_ref):
    r = _rms(hn)
    for k, (g_ref, n_ref) in enumerate(zip(gain_refs, norm_refs)):
        xf = r * g_ref[...]
        hi = xf.astype(_BF16)
        n_ref[...] = hi
        if k == 0 and lo_ref is not None:
            lo_ref[...] = (xf - hi.astype(_F32)).astype(_BF16)


def _embed_kernel(x_ref, meta_ref, g_ref, h_ref, xn_ref):
    t = pl.program_id(1)
    tile = h_ref.shape[0]

    @pl.when(t == 0)
    def _():
        pad = tile - N_META
        meta = meta_ref[...]
        h_ref[0:pad, :] = jnp.zeros((pad, h_ref.shape[1]), _F32)
        h_ref[pad:, :] = meta
        xn_ref[0:pad, :] = jnp.zeros((pad, h_ref.shape[1]), _BF16)
        xn_ref[pad:, :] = (_rms(meta) * g_ref[...]).astype(_BF16)

    @pl.when(t > 0)
    def _():
        x = x_ref[0]
        h_ref[...] = x
        xn_ref[...] = (_rms(x) * g_ref[...]).astype(_BF16)


def _embed(x, meta_tokens, g, *, tile):
    batch, seq, d = x.shape
    nt = seq // tile + 1
    return pl.pallas_call(
        _embed_kernel,
        grid=(batch, nt),
        in_specs=[
            pl.BlockSpec((1, tile, d), lambda b, t: (b, jnp.maximum(t - 1, 0), 0)),
            pl.BlockSpec((N_META, d), lambda b, t: (0, 0)),
            pl.BlockSpec((1, d), lambda b, t: (0, 0)),
        ],
        out_specs=[
            pl.BlockSpec((tile, d), lambda b, t: (b * nt + t, 0)),
            pl.BlockSpec((tile, d), lambda b, t: (b * nt + t, 0)),
        ],
        out_shape=[
            jax.ShapeDtypeStruct((batch * nt * tile, d), _F32),
            jax.ShapeDtypeStruct((batch * nt * tile, d), _BF16),
        ],
        compiler_params=_params(2),
        name="embed",
    )(x, meta_tokens, g)


def _a_in_kernel(xn_ref, wv_ref, wg_ref, wz_ref, bv_ref, bg_ref, bz_ref, y_ref, zs_ref,
                 *, rows_per_batch, front_pad):
    i = pl.program_id(0)
    x = xn_ref[...]
    tm = x.shape[0]
    val = _dot(x, wv_ref[...]) + bv_ref[...]
    gate = _dot(x, wg_ref[...]) + bg_ref[...]
    z = _dot(x, wz_ref[...]) + bz_ref[...]
    pos = lax.rem(i * tm + lax.broadcasted_iota(jnp.int32, (tm, 1), 0), rows_per_batch)
    y = jnp.where(pos >= front_pad, val * jax.nn.sigmoid(gate), 0.0)
    y_ref[...] = y.astype(_BF16)
    zs_ref[...] = (z * jax.nn.sigmoid(z)).astype(_BF16)


def _a_in(xn, w_in, b_in, *, tile, col_tile, rows_per_batch, front_pad):
    rows, d = xn.shape
    e = w_in.shape[1] // 3
    nj = e // col_tile
    kern = functools.partial(_a_in_kernel, rows_per_batch=rows_per_batch, front_pad=front_pad)
    wspec = lambda k: pl.BlockSpec((d, col_tile), lambda i, j, k=k: (0, j + k * nj))
    bspec = lambda k: pl.BlockSpec((1, col_tile), lambda i, j, k=k: (0, j + k * nj))
    return pl.pallas_call(
        kern,
        grid=(rows // tile, nj),
        in_specs=[
            pl.BlockSpec((tile, d), lambda i, j: (i, 0)),
            wspec(0), wspec(1), wspec(2),
            bspec(0), bspec(1), bspec(2),
        ],
        out_specs=[
            pl.BlockSpec((tile, col_tile), lambda i, j: (i, j)),
            pl.BlockSpec((tile, col_tile), lambda i, j: (i, j)),
        ],
        out_shape=[
            jax.ShapeDtypeStruct((rows, e), _BF16),
            jax.ShapeDtypeStruct((rows, e), _BF16),
        ],
        compiler_params=_params(2),
        name="a_in",
    )(xn, w_in, w_in, w_in, b_in, b_in, b_in)


def _a_out_kernel(*refs, n_norms, emit_lo, conv_rows, norm_rows):
    (y_ref, halo_ref, zs_ref, h_ref, dww_ref, dwb_ref, lng_ref, lnb_ref, w_ref), refs = refs[:9], refs[9:]
    gain_refs, refs = refs[:n_norms], refs[n_norms:]
    o_ref, refs = refs[0], refs[1:]
    norm_refs, refs = refs[:n_norms], refs[n_norms:]
    lo_ref = None
    if emit_lo:
        lo_ref, refs = refs[0], refs[1:]
    shift_ref, conv_ref, u_ref = refs

    tm, e = y_ref.shape
    for c in range(e // 128):
        cs = pl.ds(c * 128, 128)
        strip = jnp.concatenate([halo_ref[:, cs], y_ref[:, cs]], axis=0).astype(_F32)
        shift_ref[0] = strip
        for b in range(1, 8):
            shift_ref[b] = pltpu.roll(strip, b, 0)

        def conv_body(rb, carry, cs=cs):
            r0 = pl.multiple_of(rb * conv_rows, conv_rows)
            acc = jnp.broadcast_to(dwb_ref[:, cs], (conv_rows, 128))
            for k in range(CONV_KERNEL):
                d = CONV_KERNEL - 1 - k
                src = shift_ref[d % 8, pl.ds(r0 + CONV_HALO - 8 * (d // 8), conv_rows), :]
                acc = acc + dww_ref[pl.ds(k, 1), cs] * src
            conv_ref[pl.ds(r0, conv_rows), cs] = acc
            return carry

        lax.fori_loop(0, tm // conv_rows, conv_body, 0)

    def norm_body(rb, carry):
        r0 = pl.multiple_of(rb * norm_rows, norm_rows)
        rs = pl.ds(r0, norm_rows)
        v = conv_ref[rs, :]
        mu = jnp.mean(v, axis=-1, keepdims=True)
        vc = v - mu
        var = jnp.mean(vc * vc, axis=-1, keepdims=True)
        yn = vc * lax.rsqrt(var + EPS) * lng_ref[...] + lnb_ref[...]
        u = yn * jax.nn.sigmoid(yn) * zs_ref[rs, :].astype(_F32)
        u_ref[rs, :] = u.astype(_BF16)
        return carry

    lax.fori_loop(0, tm // norm_rows, norm_body, 0)
    hn = h_ref[...] + _dot(u_ref[...], w_ref[...])
    o_ref[...] = hn
    _emit_norms(hn, gain_refs, norm_refs, lo_ref)


def _a_out(y, zs, h, dw_w, dw_b, ln_g, ln_b, w_out, gains, *, tile, emit_lo):
    rows, e = y.shape
    d = h.shape[1]
    halo_blocks = tile // CONV_HALO
    n_norms = len(gains)
    kern = functools.partial(_a_out_kernel, n_norms=n_norms, emit_lo=emit_lo,
                             conv_rows=min(A_CONV_ROWS, tile), norm_rows=min(A_NORM_ROWS, tile))
    row = lambda i: (i, 0)
    n_bf16_out = n_norms + (1 if emit_lo else 0)
    return pl.pallas_call(
        kern,
        grid=(rows // tile,),
        in_specs=[
            pl.BlockSpec((tile, e), row),
            pl.BlockSpec((CONV_HALO, e), lambda i: (jnp.maximum(i * halo_blocks - 1, 0), 0)),
            pl.BlockSpec((tile, e), row),
            pl.BlockSpec((tile, d), row),
            _resident((CONV_KERNEL, e)),
            _resident((1, e)),
            _resident((1, e)),
            _resident((1, e)),
            _resident((e, d)),
        ] + [_resident((1, d))] * n_norms,
        out_specs=[pl.BlockSpec((tile, d), row)] * (1 + n_bf16_out),
        out_shape=[jax.ShapeDtypeStruct((rows, d), _F32)]
        + [jax.ShapeDtypeStruct((rows, d), _BF16)] * n_bf16_out,
        scratch_shapes=[
            pltpu.VMEM((8, tile + CONV_HALO, 128), _F32),
            pltpu.VMEM((tile, e), _F32),
            pltpu.VMEM((tile, e), _BF16),
        ],
        compiler_params=_params(1),
        name="a_out",
    )(y, y, zs, h, dw_w, dw_b, ln_g, ln_b, w_out, *gains)


def _kv_kernel(xn_ref, xlo_ref, wk_ref, wvt_ref, wfh_ref, wfl_ref, bf_ref, sel_ref,
               kp_ref, vt_ref, ct_ref, ccol_ref, carry_ref,
               *, rows_per_batch, front_pad, heads_per_step):
    i = pl.program_id(0)
    tm = xn_ref.shape[0]
    nh = wfh_ref.shape[0]
    row0 = i * tm
    x = xn_ref[...]

    @pl.when(pl.program_id(1) == 0)
    def _():
        ft = (_dot_nt(wfh_ref[...], x) + _dot_nt(wfh_ref[...], xlo_ref[...])
              + _dot_nt(wfl_ref[...], x) + bf_ref[...])
        ls = jnp.minimum(ft, 0.0) - jnp.log1p(jnp.exp(-jnp.abs(ft)))
        pos = lax.rem(row0 + lax.broadcasted_iota(jnp.int32, (1, tm), 1), rows_per_batch)
        ls = jnp.where(pos >= front_pad, ls, 0.0)
        carry = jnp.where(lax.rem(row0, rows_per_batch) == 0, 0.0, carry_ref[...])
        tri = (lax.broadcasted_iota(jnp.int32, (tm, tm), 0)
               <= lax.broadcasted_iota(jnp.int32, (tm, tm), 1)).astype(_BF16)
        pieces = _dot(jnp.concatenate(_split3(ls), axis=0), tri)
        c = pieces[0:nh] + pieces[nh:2 * nh] + pieces[2 * nh:3 * nh] + carry
        carry_ref[...] = c[:, tm - 1:tm]
        hi, mid, lo = [p.astype(_F32) for p in _split3(c * LOG2E)]
        r = lax.broadcasted_iota(jnp.int32, (AUG_ROWS, tm), 0)
        ones_rows = jnp.where(r < 6, 1.0, 0.0)
        for hd in range(nh):
            blk = jnp.where(r == 0, hi[hd:hd + 1],
                            jnp.where(r == 1, mid[hd:hd + 1],
                                      jnp.where(r == 2, lo[hd:hd + 1], ones_rows)))
            ct_ref[hd] = blk.astype(_BF16)
        stack = jnp.concatenate([hi, mid, lo, jnp.zeros((128 - 3 * nh, tm), _F32)], axis=0)
        ccol_ref[...] = stack.T.astype(_BF16)

    kk = _dot(x, wk_ref[...])
    ka = _dot(ccol_ref[...], sel_ref[...])
    width = ka.shape[1]
    lane = lax.broadcasted_iota(jnp.int32, (tm, width), 1) & (HEAD_DIM - 1)
    pos = lax.rem(row0 + lax.broadcasted_iota(jnp.int32, (tm, 1), 0), rows_per_batch)
    ka = jnp.where(lane < 3, 1.0, ka)
    ka = jnp.where(jnp.logical_and(lane == 3, pos < front_pad), MASKED_KEY, ka)
    vt = _dot_nt(wvt_ref[...], x)
    ones_blk = jnp.where(lax.broadcasted_iota(jnp.int32, (BF16_SUBLANES, tm), 0) == 0, 1.0, 0.0).astype(_BF16)
    for hh in range(heads_per_step):
        hs = slice(hh * HEAD_DIM, (hh + 1) * HEAD_DIM)
        kp_ref[hh, :, 0:HEAD_DIM] = kk[:, hs].astype(_BF16)
        kp_ref[hh, :, HEAD_DIM:QK_DEPTH] = ka[:, hs].astype(_BF16)
        vt_ref[hh, 0, 0:HEAD_DIM, :] = vt[hs, :].astype(_BF16)
        vt_ref[hh, 0, HEAD_DIM:V_ROWS, :] = ones_blk


def _kv(xn, xlo, wk, wvt, wfh, wfl, bf, sel, *, tile, heads_per_step, rows_per_batch, front_pad):
    rows, d = xn.shape
    nh = wfh.shape[0]
    cw = heads_per_step * HEAD_DIM
    kern = functools.partial(_kv_kernel, rows_per_batch=rows_per_batch, front_pad=front_pad,
                             heads_per_step=heads_per_step)
    return pl.pallas_call(
        kern,
        grid=(rows // tile, nh // heads_per_step),
        in_specs=[
            pl.BlockSpec((tile, d), lambda i, j: (i, 0)),
            pl.BlockSpec((tile, d), lambda i, j: (i, 0)),
            pl.BlockSpec((d, cw), lambda i, j: (0, j)),
            pl.BlockSpec((cw, d), lambda i, j: (j, 0)),
            pl.BlockSpec((nh, d), lambda i, j: (0, 0)),
            pl.BlockSpec((nh, d), lambda i, j: (0, 0)),
            pl.BlockSpec((nh, 1), lambda i, j: (0, 0)),
            pl.BlockSpec((128, cw), lambda i, j: (0, j)),
        ],
        out_specs=[
            pl.BlockSpec((heads_per_step, tile, QK_DEPTH), lambda i, j: (j, i, 0)),
            pl.BlockSpec((heads_per_step, 1, V_ROWS, tile), lambda i, j: (j, i, 0, 0)),
            pl.BlockSpec((nh, AUG_ROWS, tile), lambda i, j: (0, 0, i)),
        ],
        out_shape=[
            jax.ShapeDtypeStruct((nh, rows, QK_DEPTH), _BF16),
            jax.ShapeDtypeStruct((nh, rows // tile, V_ROWS, tile), _BF16),
            jax.ShapeDtypeStruct((nh, AUG_ROWS, rows), _BF16),
        ],
        scratch_shapes=[
            pltpu.VMEM((tile, 128), _BF16),
            pltpu.VMEM((nh, 1), _F32),
        ],
        compiler_params=_params(2),
        name="kv_proj",
    )(xn, xlo, wk, wvt, wfh, wfl, bf, sel)


def _q_kernel(xn_ref, wqt_ref, wz_ref, qt_ref, zs_ref, *, heads_per_step, q_scale):
    x = xn_ref[...]
    qt = _dot_nt(wqt_ref[...], x) * q_scale
    for hh in range(heads_per_step):
        qt_ref[hh] = qt[hh * HEAD_DIM:(hh + 1) * HEAD_DIM, :].astype(_BF16)
    z = _dot(x, wz_ref[...])
    zs_ref[...] = (z * jax.nn.sigmoid(z)).astype(_BF16)


def _q_proj(xn, wqt, wz, *, tile, heads_per_step):
    rows, d = xn.shape
    a = wz.shape[1]
    nh = a // HEAD_DIM
    cw = heads_per_step * HEAD_DIM
    kern = functools.partial(_q_kernel, heads_per_step=heads_per_step,
                             q_scale=HEAD_DIM ** -0.5 * LOG2E)
    return pl.pallas_call(
        kern,
        grid=(rows // tile, nh // heads_per_step),
        in_specs=[
            pl.BlockSpec((tile, d), lambda i, j: (i, 0)),
            pl.BlockSpec((cw, d), lambda i, j: (j, 0)),
            pl.BlockSpec((d, cw), lambda i, j: (0, j)),
        ],
        out_specs=[
            pl.BlockSpec((heads_per_step, HEAD_DIM, tile), lambda i, j: (j, 0, i)),
            pl.BlockSpec((tile, cw), lambda i, j: (i, j)),
        ],
        out_shape=[
            jax.ShapeDtypeStruct((nh, HEAD_DIM, rows), _BF16),
            jax.ShapeDtypeStruct((rows, a), _BF16),
        ],
        compiler_params=_params(2),
        name="q_proj",
    )(xn, wqt, wz)


def _attn_kernel(qt_ref, ct_ref, k_ref, vt_ref, o_ref, acc_ref, m_ref, sa_ref, sb_ref, *, pairs_per_trip):
    nq = vt_ref.shape[1]
    tq = tk = vt_ref.shape[3]

    def q_aug(qi):
        cols = pl.ds(pl.multiple_of(qi * tq, tq), tq)
        return jnp.concatenate(
            [qt_ref[0, :, cols], ct_ref[0, :, cols],
             jnp.zeros((QK_DEPTH - HEAD_DIM - AUG_ROWS, tq), _BF16)], axis=0)

    def logits(qa, j, s_ref):
        ks = k_ref[0, pl.ds(pl.multiple_of(j * tk, tk), tk), :]
        s_ref[...] = _dot(ks, qa)

    def consume(j, s_ref, on_diagonal):
        s = s_ref[...]
        if on_diagonal:
            key = lax.broadcasted_iota(jnp.int32, (tk, tq), 0)
            qry = lax.broadcasted_iota(jnp.int32, (tk, tq), 1)
            s = jnp.where(key <= qry, s, -jnp.inf)
        m_prev = m_ref[...]
        m_new = jnp.maximum(m_prev, jnp.max(s, axis=0, keepdims=True))
        p = jnp.exp2(s - m_new)
        alpha = jnp.exp2(m_prev - m_new)
        acc_ref[...] = alpha * acc_ref[...] + _dot(vt_ref[0, j], p.astype(_BF16))
        m_ref[...] = m_new

    logits(q_aug(0), 0, sa_ref)
    trip = 2 * pairs_per_trip

    def q_block(qi, carry):
        qa = q_aug(qi)
        qa_next = q_aug(jnp.minimum(qi + 1, nq - 1))
        m_ref[...] = jnp.full(m_ref.shape, -jnp.inf, _F32)
        acc_ref[...] = jnp.zeros(acc_ref.shape, _F32)

        def pairs(j, n):
            for t in range(n):
                logits(qa, j + 2 * t + 1, sb_ref)
                consume(j + 2 * t, sa_ref, False)
                logits(qa, j + 2 * t + 2, sa_ref)
                consume(j + 2 * t + 1, sb_ref, False)

        n_trips = qi // trip

        def trip_body(t, c):
            pairs(t * trip, pairs_per_trip)
            return c

        lax.fori_loop(0, n_trips, trip_body, 0)
        done = n_trips * trip
        rem = qi - done
        half_trip = trip // 2

        @pl.when(rem >= half_trip)
        def _():
            pairs(done, pairs_per_trip // 2)

        done = done + (rem & half_trip)
        rem = rem & (half_trip - 1)

        def tail(r):
            bufs = (sa_ref, sb_ref)
            for t in range(r):
                logits(qa, done + t + 1, bufs[(t + 1) % 2])
                consume(done + t, bufs[t % 2], False)
            if r % 2 == 1:
                logits(qa_next, 0, sa_ref)
                consume(qi, sb_ref, True)
            else:
                consume(qi, sa_ref, True)
                logits(qa_next, 0, sa_ref)

        for r in range(half_trip):
            @pl.when(rem == r)
            def _(r=r):
                tail(r)

        acc = acc_ref[...]
        o = acc[0:HEAD_DIM, :] / acc[HEAD_DIM:HEAD_DIM + 1, :]
        o_ref[pl.ds(pl.multiple_of(qi * tq, tq), tq), :] = o.T.astype(_BF16)
        return carry

    lax.fori_loop(0, nq, q_block, 0)


def _attention(qt, ct, kp, vt, *, batch, tile):
    nh, _, rows = qt.shape
    nq = rows // tile // batch
    rows_per_batch = nq * tile
    return pl.pallas_call(
        functools.partial(_attn_kernel, pairs_per_trip=ATTN_PAIRS_PER_TRIP),
        grid=(nh, batch),
        in_specs=[
            pl.BlockSpec((1, HEAD_DIM, rows_per_batch), lambda h, b: (h, 0, b)),
            pl.BlockSpec((1, AUG_ROWS, rows_per_batch), lambda h, b: (h, 0, b)),
            pl.BlockSpec((1, rows_per_batch, QK_DEPTH), lambda h, b: (h, b, 0)),
            pl.BlockSpec((1, nq, V_ROWS, tile), lambda h, b: (h, b, 0, 0)),
        ],
        out_specs=pl.BlockSpec((rows_per_batch, HEAD_DIM), lambda h, b: (b, h)),
        out_shape=jax.ShapeDtypeStruct((rows, nh * HEAD_DIM), _BF16),
        scratch_shapes=[
            pltpu.VMEM((V_ROWS, tile), _F32),
            pltpu.VMEM((1, tile), _F32),
            pltpu.VMEM((tile, tile), _F32),
            pltpu.VMEM((tile, tile), _F32),
        ],
        compiler_params=_params(2),
        name="fox_attention",
    )(qt, ct, kp, vt)


def _b_out_kernel(o_ref, zs_ref, h_ref, w_ref, g_ref, out_ref, xn_ref):
    hn = h_ref[...] + _dot(o_ref[...] * zs_ref[...], w_ref[...])
    out_ref[...] = hn
    _emit_norms(hn, [g_ref], [xn_ref], None)


def _b_out_final_kernel(o_ref, zs_ref, h_ref, w_ref, g_ref, out_ref):
    hn = h_ref[...] + _dot(o_ref[...] * zs_ref[...], w_ref[...])
    out_ref[...] = _rms(hn) * g_ref[...]


def _b_out(o, zs, h, w_out, g, *, tile):
    rows, a = o.shape
    d = h.shape[1]
    row = lambda i: (i, 0)
    return pl.pallas_call(
        _b_out_kernel,
        grid=(rows // tile,),
        in_specs=[
            pl.BlockSpec((tile, a), row),
            pl.BlockSpec((tile, a), row),
            pl.BlockSpec((tile, d), row),
            _resident((a, d)),
            _resident((1, d)),
        ],
        out_specs=[pl.BlockSpec((tile, d), row), pl.BlockSpec((tile, d), row)],
        out_shape=[jax.ShapeDtypeStruct((rows, d), _F32), jax.ShapeDtypeStruct((rows, d), _BF16)],
        compiler_params=_params(1),
        name="b_out",
    )(o, zs, h, w_out, g)


def _b_out_final(o, zs, h, w_out, g, *, tile, batch, seq):
    rows, a = o.shape
    d = h.shape[1]
    tiles_per_batch = rows // batch // tile
    out_tiles_per_batch = seq // tile
    row = lambda i: (i, 0)

    def out_map(i):
        b = i // tiles_per_batch
        t = i - b * tiles_per_batch
        return (b * out_tiles_per_batch + jnp.maximum(t - (tiles_per_batch - out_tiles_per_batch), 0), 0)

    return pl.pallas_call(
        _b_out_final_kernel,
        grid=(rows // tile,),
        in_specs=[
            pl.BlockSpec((tile, a), row),
            pl.BlockSpec((tile, a), row),
            pl.BlockSpec((tile, d), row),
            _resident((a, d)),
            _resident((1, d)),
        ],
        out_specs=pl.BlockSpec((tile, d), out_map),
        out_shape=jax.ShapeDtypeStruct((batch * seq, d), _F32),
        compiler_params=_params(1),
        name="b_out_final",
    )(o, zs, h, w_out, g)


def _forward(x, meta_tokens, a_norm_g, a_w_in, a_b_in, a_dw_w, a_dw_b, a_ln_g, a_ln_b, a_w_out,
             kv_norm_g, w_kvf, b_f, b_norm_g, b_w_in, b_w_out, final_norm_g,
             *, tile, a_in_rows, a_out_rows):
    batch, seq, d = x.shape
    n_a = a_w_in.shape[0]
    n_b = b_w_in.shape[0]
    attn_width = b_w_out.shape[1]
    nh = attn_width // HEAD_DIM
    heads_per_step = min(4, nh)
    assert seq % tile == 0 and N_META <= tile and a_out_rows % CONV_HALO == 0
    assert nh % heads_per_step == 0 and 3 * nh <= 128 and n_a >= 1 and n_b >= 1
    rows_per_batch = seq + tile
    front_pad = tile - N_META
    rows = batch * rows_per_batch
    assert rows % a_in_rows == 0 and rows % a_out_rows == 0
    col_tile = min(512, a_w_in.shape[2] // 3)

    row_vec = lambda v: v.reshape(1, -1).astype(_F32)
    x = x.astype(_F32)
    h, xn = _embed(x, meta_tokens.astype(_F32), row_vec(a_norm_g[0]), tile=tile)

    xn_kv = xlo_kv = None
    for layer in range(n_a):
        y, zs = _a_in(xn, a_w_in[layer].astype(_BF16), row_vec(a_b_in[layer]), tile=a_in_rows,
                      col_tile=col_tile, rows_per_batch=rows_per_batch, front_pad=front_pad)
        last = layer + 1 == n_a
        gains = [row_vec(kv_norm_g), row_vec(b_norm_g[0])] if last else [row_vec(a_norm_g[layer + 1])]
        outs = _a_out(y, zs, h, a_dw_w[layer].astype(_F32), row_vec(a_dw_b[layer]), row_vec(a_ln_g[layer]),
                      row_vec(a_ln_b[layer]), a_w_out[layer].astype(_BF16), gains,
                      tile=a_out_rows, emit_lo=last)
        if last:
            h, xn_kv, xn, xlo_kv = outs
        else:
            h, xn = outs

    wk = w_kvf[:, :attn_width].astype(_BF16)
    wvt = w_kvf[:, attn_width:2 * attn_width].T.astype(_BF16)
    wft = w_kvf[:, 2 * attn_width:].T.astype(_F32)
    wfh = wft.astype(_BF16)
    wfl = (wft - wfh.astype(_F32)).astype(_BF16)
    piece = jnp.arange(128)[:, None]
    col = jnp.arange(attn_width)[None, :]
    sel = jnp.where((piece < 3 * nh) & (col // HEAD_DIM == piece % nh)
                    & (col % HEAD_DIM == 3 + piece // nh), -1.0, 0.0).astype(_BF16)
    kp, vt, ct = _kv(xn_kv, xlo_kv, wk, wvt, wfh, wfl, b_f.reshape(nh, 1).astype(_F32), sel,
                     tile=tile, heads_per_step=heads_per_step,
                     rows_per_batch=rows_per_batch, front_pad=front_pad)

    for layer in range(n_b):
        wqt = b_w_in[layer][:, :attn_width].T.astype(_BF16)
        wz = b_w_in[layer][:, attn_width:].astype(_BF16)
        qt, zs = _q_proj(xn, wqt, wz, tile=a_in_rows, heads_per_step=heads_per_step)
        o = _attention(qt, ct, kp, vt, batch=batch, tile=tile)
        w_out = b_w_out[layer].astype(_BF16)
        if layer + 1 < n_b:
            h, xn = _b_out(o, zs, h, w_out, row_vec(b_norm_g[layer + 1]), tile=tile)
        else:
            h = _b_out_final(o, zs, h, w_out, row_vec(final_norm_g), tile=tile, batch=batch, seq=seq)
    return h.reshape(batch, seq, d)


def kernel(x, meta_tokens, a_norm_g, a_w_in, a_b_in, a_dw_w, a_dw_b, a_ln_g, a_ln_b, a_w_out,
           kv_norm_g, w_kvf, b_f, b_norm_g, b_w_in, b_w_out, final_norm_g):
    return _forward(x, meta_tokens, a_norm_g, a_w_in, a_b_in, a_dw_w, a_dw_b, a_ln_g, a_ln_b, a_w_out,
                    kv_norm_g, w_kvf, b_f, b_norm_g, b_w_in, b_w_out, final_norm_g,
                    tile=SEQ_TILE, a_in_rows=A_IN_ROWS, a_out_rows=A_OUT_ROWS)
```

```python
import functools

import jax
import jax.numpy as jnp
from jax import lax
from jax.experimental import pallas as pl
from jax.experimental.pallas import tpu as pltpu

N_META = 16
HEAD_DIM = 128
CONV_KERNEL = 31
EPS = 1e-6

SEQ_TILE = 512
A_IN_ROWS = 1024
A_OUT_ROWS = 256
A_CONV_ROWS = 256
A_NORM_ROWS = 256
ATTN_PAIRS_PER_TRIP = 8
CONV_HALO = 32
BF16_SUBLANES = 16
AUG_ROWS = BF16_SUBLANES
V_ROWS = HEAD_DIM + BF16_SUBLANES
QK_DEPTH = 2 * HEAD_DIM
LOG2E = 1.4426950408889634
MASKED_KEY = -1e30
VMEM_LIMIT_BYTES = 56 * 1024 * 1024

_F32 = jnp.float32
_BF16 = jnp.bfloat16


def _dot(a, b):
    return jnp.dot(a, b, preferred_element_type=_F32)


def _dot_nt(a, b):
    return lax.dot_general(a, b, (((1,), (1,)), ((), ())), preferred_element_type=_F32)


def _rms(x):
    ms = jnp.mean(x * x, axis=-1, keepdims=True)
    return x * lax.rsqrt(ms + EPS)


def _split3(x):
    hi = x.astype(_BF16)
    r = x - hi.astype(_F32)
    mid = r.astype(_BF16)
    lo = (r - mid.astype(_F32)).astype(_BF16)
    return hi, mid, lo


def _params(n_grid_dims):
    return pltpu.CompilerParams(
        dimension_semantics=("arbitrary",) * n_grid_dims,
        vmem_limit_bytes=VMEM_LIMIT_BYTES,
    )


def _resident(shape):
    return pl.BlockSpec(shape, lambda *_: (0,) * len(shape), pipeline_mode=pl.Buffered(1))


def _emit_norms(hn, gain_refs, norm_refs, lo_ref):
    r = _rms(hn)
    for k, (g_ref, n_ref) in enumerate(zip(gain_refs, norm_refs)):
        xf = r * g_ref[...]
        hi = xf.astype(_BF16)
        n_ref[...] = hi
        if k == 0 and lo_ref is not None:
            lo_ref[...] = (xf - hi.astype(_F32)).astype(_BF16)


def _embed_kernel(x_ref, meta_ref, g_ref, h_ref, xn_ref):
    t = pl.program_id(1)
    tile = h_ref.shape[0]

    @pl.when(t == 0)
    def _():
        pad = tile - N_META
        meta = meta_ref[...]
        h_ref[0:pad, :] = jnp.zeros((pad, h_ref.shape[1]), _F32)
        h_ref[pad:, :] = meta
        xn_ref[0:pad, :] = jnp.zeros((pad, h_ref.shape[1]), _BF16)
        xn_ref[pad:, :] = (_rms(meta) * g_ref[...]).astype(_BF16)

    @pl.when(t > 0)
    def _():
        x = x_ref[0]
        h_ref[...] = x
        xn_ref[...] = (_rms(x) * g_ref[...]).astype(_BF16)


def _embed(x, meta_tokens, g, *, tile):
    batch, seq, d = x.shape
    nt = seq // tile + 1
    return pl.pallas_call(
        _embed_kernel,
        grid=(batch, nt),
        in_specs=[
            pl.BlockSpec((1, tile, d), lambda b, t: (b, jnp.maximum(t - 1, 0), 0)),
            pl.BlockSpec((N_META, d), lambda b, t: (0, 0)),
            pl.BlockSpec((1, d), lambda b, t: (0, 0)),
        ],
        out_specs=[
            pl.BlockSpec((tile, d), lambda b, t: (b * nt + t, 0)),
            pl.BlockSpec((tile, d), lambda b, t: (b * nt + t, 0)),
        ],
        out_shape=[
            jax.ShapeDtypeStruct((batch * nt * tile, d), _F32),
            jax.ShapeDtypeStruct((batch * nt * tile, d), _BF16),
        ],
        compiler_params=_params(2),
        name="embed",
    )(x, meta_tokens, g)


def _a_in_kernel(xn_ref, wv_ref, wg_ref, wz_ref, bv_ref, bg_ref, bz_ref, y_ref, zs_ref,
                 *, rows_per_batch, front_pad):
    i = pl.program_id(0)
    x = xn_ref[...]
    tm = x.shape[0]
    pos = lax.rem(i * tm + lax.broadcasted_iota(jnp.int32, (tm, 1), 0), rows_per_batch)
    visible = pos >= front_pad
    half = wv_ref.shape[1] // 2
    for k in range(2):
        cs = slice(k * half, (k + 1) * half)
        val = _dot(x, wv_ref[:, cs]) + bv_ref[:, cs]
        gate = _dot(x, wg_ref[:, cs]) + bg_ref[:, cs]
        z = _dot(x, wz_ref[:, cs]) + bz_ref[:, cs]
        y_ref[:, cs] = jnp.where(visible, val * jax.nn.sigmoid(gate), 0.0).astype(_BF16)
        zs_ref[:, cs] = (z * jax.nn.sigmoid(z)).astype(_BF16)


def _a_in(xn, w_in, b_in, *, tile, col_tile, rows_per_batch, front_pad):
    rows, d = xn.shape
    e = w_in.shape[1] // 3
    nj = e // col_tile
    kern = functools.partial(_a_in_kernel, rows_per_batch=rows_per_batch, front_pad=front_pad)
    wspec = lambda k: pl.BlockSpec((d, col_tile), lambda i, j, k=k: (0, j + k * nj))
    bspec = lambda k: pl.BlockSpec((1, col_tile), lambda i, j, k=k: (0, j + k * nj))
    return pl.pallas_call(
        kern,
        grid=(rows // tile, nj),
        in_specs=[
            pl.BlockSpec((tile, d), lambda i, j: (i, 0)),
            wspec(0), wspec(1), wspec(2),
            bspec(0), bspec(1), bspec(2),
        ],
        out_specs=[
            pl.BlockSpec((tile, col_tile), lambda i, j: (i, j)),
            pl.BlockSpec((tile, col_tile), lambda i, j: (i, j)),
        ],
        out_shape=[
            jax.ShapeDtypeStruct((rows, e), _BF16),
            jax.ShapeDtypeStruct((rows, e), _BF16),
        ],
        compiler_params=_params(2),
        name="a_in",
    )(xn, w_in, w_in, w_in, b_in, b_in, b_in)


def _a_out_kernel(*refs, n_norms, emit_lo, conv_rows, norm_rows):
    (y_ref, halo_ref, zs_ref, h_ref, dww_ref, dwb_ref, lng_ref, lnb_ref, w_ref), refs = refs[:9], refs[9:]
    gain_refs, refs = refs[:n_norms], refs[n_norms:]
    o_ref, refs = refs[0], refs[1:]
    norm_refs, refs = refs[:n_norms], refs[n_norms:]
    lo_ref = None
    if emit_lo:
        lo_ref, refs = refs[0], refs[1:]
    shift_ref, conv_ref, u_ref = refs

    tm, e = y_ref.shape
    for c in range(e // 128):
        cs = pl.ds(c * 128, 128)
        strip = jnp.concatenate([halo_ref[:, cs], y_ref[:, cs]], axis=0).astype(_F32)
        shift_ref[0] = strip
        for b in range(1, 8):
            shift_ref[b] = pltpu.roll(strip, b, 0)

        def conv_body(rb, carry, cs=cs):
            r0 = pl.multiple_of(rb * conv_rows, conv_rows)
            acc = jnp.broadcast_to(dwb_ref[:, cs], (conv_rows, 128))
            for k in range(CONV_KERNEL):
                d = CONV_KERNEL - 1 - k
                src = shift_ref[d % 8, pl.ds(r0 + CONV_HALO - 8 * (d // 8), conv_rows), :]
                acc = acc + dww_ref[pl.ds(k, 1), cs] * src
            conv_ref[pl.ds(r0, conv_rows), cs] = acc
            return carry

        lax.fori_loop(0, tm // conv_rows, conv_body, 0)

    def norm_body(rb, carry):
        r0 = pl.multiple_of(rb * norm_rows, norm_rows)
        rs = pl.ds(r0, norm_rows)
        v = conv_ref[rs, :]
        mu = jnp.mean(v, axis=-1, keepdims=True)
        vc = v - mu
        var = jnp.mean(vc * vc, axis=-1, keepdims=True)
        yn = vc * lax.rsqrt(var + EPS) * lng_ref[...] + lnb_ref[...]
        u = yn * jax.nn.sigmoid(yn) * zs_ref[rs, :].astype(_F32)
        u_ref[rs, :] = u.astype(_BF16)
        return carry

    lax.fori_loop(0, tm // norm_rows, norm_body, 0)
    hn = h_ref[...] + _dot(u_ref[...], w_ref[...])
    o_ref[...] = hn
    _emit_norms(hn, gain_refs, norm_refs, lo_ref)


def _a_out(y, zs, h, dw_w, dw_b, ln_g, ln_b, w_out, gains, *, tile, emit_lo):
    rows, e = y.shape
    d = h.shape[1]
    halo_blocks = tile // CONV_HALO
    n_norms = len(gains)
    kern = functools.partial(_a_out_kernel, n_norms=n_norms, emit_lo=emit_lo,
                             conv_rows=min(A_CONV_ROWS, tile), norm_rows=min(A_NORM_ROWS, tile))
    row = lambda i: (i, 0)
    n_bf16_out = n_norms + (1 if emit_lo else 0)
    return pl.pallas_call(
        kern,
        grid=(rows // tile,),
        in_specs=[
            pl.BlockSpec((tile, e), row),
            pl.BlockSpec((CONV_HALO, e), lambda i: (jnp.maximum(i * halo_blocks - 1, 0), 0)),
            pl.BlockSpec((tile, e), row),
            pl.BlockSpec((tile, d), row),
            _resident((CONV_KERNEL, e)),
            _resident((1, e)),
            _resident((1, e)),
            _resident((1, e)),
            _resident((e, d)),
        ] + [_resident((1, d))] * n_norms,
        out_specs=[pl.BlockSpec((tile, d), row)] * (1 + n_bf16_out),
        out_shape=[jax.ShapeDtypeStruct((rows, d), _F32)]
        + [jax.ShapeDtypeStruct((rows, d), _BF16)] * n_bf16_out,
        scratch_shapes=[
            pltpu.VMEM((8, tile + CONV_HALO, 128), _F32),
            pltpu.VMEM((tile, e), _F32),
            pltpu.VMEM((tile, e), _BF16),
        ],
        compiler_params=_params(1),
        name="a_out",
    )(y, y, zs, h, dw_w, dw_b, ln_g, ln_b, w_out, *gains)


def _kv_kernel(xn_ref, xlo_ref, wk_ref, wvt_ref, wfh_ref, wfl_ref, bf_ref, sel_ref,
               kp_ref, vt_ref, ct_ref, ccol_ref, carry_ref,
               *, rows_per_batch, front_pad, heads_per_step):
    i = pl.program_id(0)
    tm = xn_ref.shape[0]
    nh = wfh_ref.shape[0]
    row0 = i * tm
    x = xn_ref[...]

    @pl.when(pl.program_id(1) == 0)
    def _():
        ft = (_dot_nt(wfh_ref[...], x) + _dot_nt(wfh_ref[...], xlo_ref[...])
              + _dot_nt(wfl_ref[...], x) + bf_ref[...])
        ls = jnp.minimum(ft, 0.0) - jnp.log1p(jnp.exp(-jnp.abs(ft)))
        pos = lax.rem(row0 + lax.broadcasted_iota(jnp.int32, (1, tm), 1), rows_per_batch)
        ls = jnp.where(pos >= front_pad, ls, 0.0)
        carry = jnp.where(lax.rem(row0, rows_per_batch) == 0, 0.0, carry_ref[...])
        tri = (lax.broadcasted_iota(jnp.int32, (tm, tm), 0)
               <= lax.broadcasted_iota(jnp.int32, (tm, tm), 1)).astype(_BF16)
        pieces = _dot(jnp.concatenate(_split3(ls), axis=0), tri)
        c = pieces[0:nh] + pieces[nh:2 * nh] + pieces[2 * nh:3 * nh] + carry
        carry_ref[...] = c[:, tm - 1:tm]
        hi, mid, lo = [p.astype(_F32) for p in _split3(c * LOG2E)]
        r = lax.broadcasted_iota(jnp.int32, (AUG_ROWS, tm), 0)
        ones_rows = jnp.where(r < 6, 1.0, 0.0)
        for hd in range(nh):
            blk = jnp.where(r == 0, hi[hd:hd + 1],
                            jnp.where(r == 1, mid[hd:hd + 1],
                                      jnp.where(r == 2, lo[hd:hd + 1], ones_rows)))
            ct_ref[hd] = blk.astype(_BF16)
        stack = jnp.concatenate([hi, mid, lo, jnp.zeros((128 - 3 * nh, tm), _F32)], axis=0)
        ccol_ref[...] = stack.T.astype(_BF16)

    kk = _dot(x, wk_ref[...])
    ka = _dot(ccol_ref[...], sel_ref[...])
    width = ka.shape[1]
    lane = lax.broadcasted_iota(jnp.int32, (tm, width), 1) & (HEAD_DIM - 1)
    pos = lax.rem(row0 + lax.broadcasted_iota(jnp.int32, (tm, 1), 0), rows_per_batch)
    ka = jnp.where(lane < 3, 1.0, ka)
    ka = jnp.where(jnp.logical_and(lane == 3, pos < front_pad), MASKED_KEY, ka)
    vt = _dot_nt(wvt_ref[...], x)
    ones_blk = jnp.where(lax.broadcasted_iota(jnp.int32, (BF16_SUBLANES, tm), 0) == 0, 1.0, 0.0).astype(_BF16)
    for hh in range(heads_per_step):
        hs = slice(hh * HEAD_DIM, (hh + 1) * HEAD_DIM)
        kp_ref[hh, :, 0:HEAD_DIM] = kk[:, hs].astype(_BF16)
        kp_ref[hh, :, HEAD_DIM:QK_DEPTH] = ka[:, hs].astype(_BF16)
        vt_ref[hh, 0, 0:HEAD_DIM, :] = vt[hs, :].astype(_BF16)
        vt_ref[hh, 0, HEAD_DIM:V_ROWS, :] = ones_blk


def _kv(xn, xlo, wk, wvt, wfh, wfl, bf, sel, *, tile, heads_per_step, rows_per_batch, front_pad):
    rows, d = xn.shape
    nh = wfh.shape[0]
    cw = heads_per_step * HEAD_DIM
    kern = functools.partial(_kv_kernel, rows_per_batch=rows_per_batch, front_pad=front_pad,
                             heads_per_step=heads_per_step)
    return pl.pallas_call(
        kern,
        grid=(rows // tile, nh // heads_per_step),
        in_specs=[
            pl.BlockSpec((tile, d), lambda i, j: (i, 0)),
            pl.BlockSpec((tile, d), lambda i, j: (i, 0)),
            pl.BlockSpec((d, cw), lambda i, j: (0, j)),
            pl.BlockSpec((cw, d), lambda i, j: (j, 0)),
            pl.BlockSpec((nh, d), lambda i, j: (0, 0)),
            pl.BlockSpec((nh, d), lambda i, j: (0, 0)),
            pl.BlockSpec((nh, 1), lambda i, j: (0, 0)),
            pl.BlockSpec((128, cw), lambda i, j: (0, j)),
        ],
        out_specs=[
            pl.BlockSpec((heads_per_step, tile, QK_DEPTH), lambda i, j: (j, i, 0)),
            pl.BlockSpec((heads_per_step, 1, V_ROWS, tile), lambda i, j: (j, i, 0, 0)),
            pl.BlockSpec((nh, AUG_ROWS, tile), lambda i, j: (0, 0, i)),
        ],
        out_shape=[
            jax.ShapeDtypeStruct((nh, rows, QK_DEPTH), _BF16),
            jax.ShapeDtypeStruct((nh, rows // tile, V_ROWS, tile), _BF16),
            jax.ShapeDtypeStruct((nh, AUG_ROWS, rows), _BF16),
        ],
        scratch_shapes=[
            pltpu.VMEM((tile, 128), _BF16),
            pltpu.VMEM((nh, 1), _F32),
        ],
        compiler_params=_params(2),
        name="kv_proj",
    )(xn, xlo, wk, wvt, wfh, wfl, bf, sel)


def _q_kernel(xn_ref, wqt_ref, wz_ref, qt_ref, zs_ref, *, heads_per_step, q_scale):
    x = xn_ref[...]
    qt = _dot_nt(wqt_ref[...], x) * q_scale
    for hh in range(heads_per_step):
        qt_ref[hh] = qt[hh * HEAD_DIM:(hh + 1) * HEAD_DIM, :].astype(_BF16)
    z = _dot(x, wz_ref[...])
    zs_ref[...] = (z * jax.nn.sigmoid(z)).astype(_BF16)


def _q_proj(xn, wqt, wz, *, tile, heads_per_step):
    rows, d = xn.shape
    a = wz.shape[1]
    nh = a // HEAD_DIM
    cw = heads_per_step * HEAD_DIM
    kern = functools.partial(_q_kernel, heads_per_step=heads_per_step,
                             q_scale=HEAD_DIM ** -0.5 * LOG2E)
    return pl.pallas_call(
        kern,
        grid=(rows // tile, nh // heads_per_step),
        in_specs=[
            pl.BlockSpec((tile, d), lambda i, j: (i, 0)),
            pl.BlockSpec((cw, d), lambda i, j: (j, 0)),
            pl.BlockSpec((d, cw), lambda i, j: (0, j)),
        ],
        out_specs=[
            pl.BlockSpec((heads_per_step, HEAD_DIM, tile), lambda i, j: (j, 0, i)),
            pl.BlockSpec((tile, cw), lambda i, j: (i, j)),
        ],
        out_shape=[
            jax.ShapeDtypeStruct((nh, HEAD_DIM, rows), _BF16),
            jax.ShapeDtypeStruct((rows, a), _BF16),
        ],
        compiler_params=_params(2),
        name="q_proj",
    )(xn, wqt, wz)


def _attn_kernel(qt_ref, ct_ref, k_ref, vt_ref, o_ref, acc_ref, m_ref, sa_ref, sb_ref, *, pairs_per_trip):
    nq = vt_ref.shape[1]
    tq = tk = vt_ref.shape[3]

    def q_aug(qi):
        cols = pl.ds(pl.multiple_of(qi * tq, tq), tq)
        return jnp.concatenate(
            [qt_ref[0, :, cols], ct_ref[0, :, cols],
             jnp.zeros((QK_DEPTH - HEAD_DIM - AUG_ROWS, tq), _BF16)], axis=0)

    def logits(qa, j, s_ref):
        ks = k_ref[0, pl.ds(pl.multiple_of(j * tk, tk), tk), :]
        s_ref[...] = _dot(ks, qa)

    def consume(j, s_ref, on_diagonal):
        s = s_ref[...]
        if on_diagonal:
            key = lax.broadcasted_iota(jnp.int32, (tk, tq), 0)
            qry = lax.broadcasted_iota(jnp.int32, (tk, tq), 1)
            s = jnp.where(key <= qry, s, -jnp.inf)
        m_prev = m_ref[...]
        m_new = jnp.maximum(m_prev, jnp.max(s, axis=0, keepdims=True))
        p = jnp.exp2(s - m_new)
        alpha = jnp.exp2(m_prev - m_new)
        acc_ref[...] = alpha * acc_ref[...] + _dot(vt_ref[0, j], p.astype(_BF16))
        m_ref[...] = m_new

    logits(q_aug(0), 0, sa_ref)
    trip = 2 * pairs_per_trip

    def q_block(qi, carry):
        qa = q_aug(qi)
        qa_next = q_aug(jnp.minimum(qi + 1, nq - 1))
        m_ref[...] = jnp.full(m_ref.shape, -jnp.inf, _F32)
        acc_ref[...] = jnp.zeros(acc_ref.shape, _F32)

        def pairs(j, n):
            for t in range(n):
                logits(qa, j + 2 * t + 1, sb_ref)
                consume(j + 2 * t, sa_ref, False)
                logits(qa, j + 2 * t + 2, sa_ref)
                consume(j + 2 * t + 1, sb_ref, False)

        n_trips = qi // trip

        def trip_body(t, c):
            pairs(t * trip, pairs_per_trip)
            return c

        lax.fori_loop(0, n_trips, trip_body, 0)
        done = n_trips * trip
        rem = qi - done
        half_trip = trip // 2

        @pl.when(rem >= half_trip)
        def _():
            pairs(done, pairs_per_trip // 2)

        done = done + (rem & half_trip)
        rem = rem & (half_trip - 1)

        def tail(r):
            bufs = (sa_ref, sb_ref)
            for t in range(r):
                logits(qa, done + t + 1, bufs[(t + 1) % 2])
                consume(done + t, bufs[t % 2], False)
            if r % 2 == 1:
                logits(qa_next, 0, sa_ref)
                consume(qi, sb_ref, True)
            else:
                consume(qi, sa_ref, True)
                logits(qa_next, 0, sa_ref)

        for r in range(half_trip):
            @pl.when(rem == r)
            def _(r=r):
                tail(r)

        acc = acc_ref[...]
        o = acc[0:HEAD_DIM, :] / acc[HEAD_DIM:HEAD_DIM + 1, :]
        o_ref[pl.ds(pl.multiple_of(qi * tq, tq), tq), :] = o.T.astype(_BF16)
        return carry

    lax.fori_loop(0, nq, q_block, 0)


def _attention(qt, ct, kp, vt, *, batch, tile):
    nh, _, rows = qt.shape
    nq = rows // tile // batch
    rows_per_batch = nq * tile
    return pl.pallas_call(
        functools.partial(_attn_kernel, pairs_per_trip=ATTN_PAIRS_PER_TRIP),
        grid=(nh, batch),
        in_specs=[
            pl.BlockSpec((1, HEAD_DIM, rows_per_batch), lambda h, b: (h, 0, b)),
            pl.BlockSpec((1, AUG_ROWS, rows_per_batch), lambda h, b: (h, 0, b)),
            pl.BlockSpec((1, rows_per_batch, QK_DEPTH), lambda h, b: (h, b, 0)),
            pl.BlockSpec((1, nq, V_ROWS, tile), lambda h, b: (h, b, 0, 0)),
        ],
        out_specs=pl.BlockSpec((rows_per_batch, HEAD_DIM), lambda h, b: (b, h)),
        out_shape=jax.ShapeDtypeStruct((rows, nh * HEAD_DIM), _BF16),
        scratch_shapes=[
            pltpu.VMEM((V_ROWS, tile), _F32),
            pltpu.VMEM((1, tile), _F32),
            pltpu.VMEM((tile, tile), _F32),
            pltpu.VMEM((tile, tile), _F32),
        ],
        compiler_params=_params(2),
        name="fox_attention",
    )(qt, ct, kp, vt)


def _b_out_kernel(o_ref, zs_ref, h_ref, w_ref, g_ref, out_ref, xn_ref):
    hn = h_ref[...] + _dot(o_ref[...] * zs_ref[...], w_ref[...])
    out_ref[...] = hn
    _emit_norms(hn, [g_ref], [xn_ref], None)


def _b_out_final_kernel(o_ref, zs_ref, h_ref, w_ref, g_ref, out_ref):
    hn = h_ref[...] + _dot(o_ref[...] * zs_ref[...], w_ref[...])
    out_ref[...] = _rms(hn) * g_ref[...]


def _b_out(o, zs, h, w_out, g, *, tile):
    rows, a = o.shape
    d = h.shape[1]
    row = lambda i: (i, 0)
    return pl.pallas_call(
        _b_out_kernel,
        grid=(rows // tile,),
        in_specs=[
            pl.BlockSpec((tile, a), row),
            pl.BlockSpec((tile, a), row),
            pl.BlockSpec((tile, d), row),
            _resident((a, d)),
            _resident((1, d)),
        ],
        out_specs=[pl.BlockSpec((tile, d), row), pl.BlockSpec((tile, d), row)],
        out_shape=[jax.ShapeDtypeStruct((rows, d), _F32), jax.ShapeDtypeStruct((rows, d), _BF16)],
        compiler_params=_params(1),
        name="b_out",
    )(o, zs, h, w_out, g)


def _b_out_final(o, zs, h, w_out, g, *, tile, batch, seq):
    rows, a = o.shape
    d = h.shape[1]
    tiles_per_batch = rows // batch // tile
    out_tiles_per_batch = seq // tile
    row = lambda i: (i, 0)

    def out_map(i):
        b = i // tiles_per_batch
        t = i - b * tiles_per_batch
        return (b * out_tiles_per_batch + jnp.maximum(t - (tiles_per_batch - out_tiles_per_batch), 0), 0)

    return pl.pallas_call(
        _b_out_final_kernel,
        grid=(rows // tile,),
        in_specs=[
            pl.BlockSpec((tile, a), row),
            pl.BlockSpec((tile, a), row),
            pl.BlockSpec((tile, d), row),
            _resident((a, d)),
            _resident((1, d)),
        ],
        out_specs=pl.BlockSpec((tile, d), out_map),
        out_shape=jax.ShapeDtypeStruct((batch * seq, d), _F32),
        compiler_params=_params(1),
        name="b_out_final",
    )(o, zs, h, w_out, g)


def _forward(x, meta_tokens, a_norm_g, a_w_in, a_b_in, a_dw_w, a_dw_b, a_ln_g, a_ln_b, a_w_out,
             kv_norm_g, w_kvf, b_f, b_norm_g, b_w_in, b_w_out, final_norm_g,
             *, tile, a_in_rows, a_out_rows):
    batch, seq, d = x.shape
    n_a = a_w_in.shape[0]
    n_b = b_w_in.shape[0]
    attn_width = b_w_out.shape[1]
    nh = attn_width // HEAD_DIM
    heads_per_step = min(4, nh)
    assert seq % tile == 0 and N_META <= tile and a_out_rows % CONV_HALO == 0
    assert nh % heads_per_step == 0 and 3 * nh <= 128 and n_a >= 1 and n_b >= 1
    rows_per_batch = seq + tile
    front_pad = tile - N_META
    rows = batch * rows_per_batch
    assert rows % a_in_rows == 0 and rows % a_out_rows == 0
    col_tile = min(512, a_w_in.shape[2] // 3)

    row_vec = lambda v: v.reshape(1, -1).astype(_F32)
    x = x.astype(_F32)
    h, xn = _embed(x, meta_tokens.astype(_F32), row_vec(a_norm_g[0]), tile=tile)

    xn_kv = xlo_kv = None
    for layer in range(n_a):
        y, zs = _a_in(xn, a_w_in[layer].astype(_BF16), row_vec(a_b_in[layer]), tile=a_in_rows,
                      col_tile=col_tile, rows_per_batch=rows_per_batch, front_pad=front_pad)
        last = layer + 1 == n_a
        gains = [row_vec(kv_norm_g), row_vec(b_norm_g[0])] if last else [row_vec(a_norm_g[layer + 1])]
        outs = _a_out(y, zs, h, a_dw_w[layer].astype(_F32), row_vec(a_dw_b[layer]), row_vec(a_ln_g[layer]),
                      row_vec(a_ln_b[layer]), a_w_out[layer].astype(_BF16), gains,
                      tile=a_out_rows, emit_lo=last)
        if last:
            h, xn_kv, xn, xlo_kv = outs
        else:
            h, xn = outs

    wk = w_kvf[:, :attn_width].astype(_BF16)
    wvt = w_kvf[:, attn_width:2 * attn_width].T.astype(_BF16)
    wft = w_kvf[:, 2 * attn_width:].T.astype(_F32)
    wfh = wft.astype(_BF16)
    wfl = (wft - wfh.astype(_F32)).astype(_BF16)
    piece = jnp.arange(128)[:, None]
    col = jnp.arange(attn_width)[None, :]
    sel = jnp.where((piece < 3 * nh) & (col // HEAD_DIM == piece % nh)
                    & (col % HEAD_DIM == 3 + piece // nh), -1.0, 0.0).astype(_BF16)
    kp, vt, ct = _kv(xn_kv, xlo_kv, wk, wvt, wfh, wfl, b_f.reshape(nh, 1).astype(_F32), sel,
                     tile=tile, heads_per_step=heads_per_step,
                     rows_per_batch=rows_per_batch, front_pad=front_pad)

    for layer in range(n_b):
        wqt = b_w_in[layer][:, :attn_width].T.astype(_BF16)
        wz = b_w_in[layer][:, attn_width:].astype(_BF16)
        qt, zs = _q_proj(xn, wqt, wz, tile=a_in_rows, heads_per_step=heads_per_step)
        o = _attention(qt, ct, kp, vt, batch=batch, tile=tile)
        w_out = b_w_out[layer].astype(_BF16)
        if layer + 1 < n_b:
            h, xn = _b_out(o, zs, h, w_out, row_vec(b_norm_g[layer + 1]), tile=tile)
        else:
            h = _b_out_final(o, zs, h, w_out, row_vec(final_norm_g), tile=tile, batch=batch, seq=seq)
    return h.reshape(batch, seq, d)


def kernel(x, meta_tokens, a_norm_g, a_w_in, a_b_in, a_dw_w, a_dw_b, a_ln_g, a_ln_b, a_w_out,
           kv_norm_g, w_kvf, b_f, b_norm_g, b_w_in, b_w_out, final_norm_g):
    return _forward(x, meta_tokens, a_norm_g, a_w_in, a_b_in, a_dw_w, a_dw_b, a_ln_g, a_ln_b, a_w_out,
                    kv_norm_g, w_kvf, b_f, b_norm_g, b_w_in, b_w_out, final_norm_g,
                    tile=SEQ_TILE, a_in_rows=A_IN_ROWS, a_out_rows=A_OUT_ROWS)
```

```python
import functools

import jax
import jax.numpy as jnp
from jax import lax
from jax.experimental import pallas as pl
from jax.experimental.pallas import tpu as pltpu

N_META = 16
HEAD_DIM = 128
CONV_KERNEL = 31
EPS = 1e-6

SEQ_TILE = 512
A_IN_ROWS = 1024
A_OUT_ROWS = 256
A_CONV_ROWS = 256
A_NORM_ROWS = 256
ATTN_PAIRS_PER_TRIP = 8
CONV_HALO = 32
BF16_SUBLANES = 16
AUG_ROWS = BF16_SUBLANES
V_ROWS = HEAD_DIM + BF16_SUBLANES
QK_DEPTH = 2 * HEAD_DIM
LOG2E = 1.4426950408889634
MASKED_KEY = -1e30
VMEM_LIMIT_BYTES = 56 * 1024 * 1024

_F32 = jnp.float32
_BF16 = jnp.bfloat16


def _dot(a, b):
    return jnp.dot(a, b, preferred_element_type=_F32)


def _dot_nt(a, b):
    return lax.dot_general(a, b, (((1,), (1,)), ((), ())), preferred_element_type=_F32)


def _rms(x):
    ms = jnp.mean(x * x, axis=-1, keepdims=True)
    return x * lax.rsqrt(ms + EPS)


def _split3(x):
    hi = x.astype(_BF16)
    r = x - hi.astype(_F32)
    mid = r.astype(_BF16)
    lo = (r - mid.astype(_F32)).astype(_BF16)
    return hi, mid, lo


def _params(n_grid_dims):
    return pltpu.CompilerParams(
        dimension_semantics=("arbitrary",) * n_grid_dims,
        vmem_limit_bytes=VMEM_LIMIT_BYTES,
    )


def _resident(shape):
    return pl.BlockSpec(shape, lambda *_: (0,) * len(shape), pipeline_mode=pl.Buffered(1))


def _emit_norms(hn, gain_refs, norm_refs, lo_ref):
    r = _rms(hn)
    for k, (g_ref, n_ref) in enumerate(zip(gain_refs, norm_refs)):
        xf = r * g_ref[...]
        hi = xf.astype(_BF16)
        n_ref[...] = hi
        if k == 0 and lo_ref is not None:
            lo_ref[...] = (xf - hi.astype(_F32)).astype(_BF16)


def _embed_kernel(x_ref, meta_ref, g_ref, h_ref, xn_ref):
    t = pl.program_id(1)
    tile = h_ref.shape[0]

    @pl.when(t == 0)
    def _():
        pad = tile - N_META
        meta = meta_ref[...]
        h_ref[0:pad, :] = jnp.zeros((pad, h_ref.shape[1]), _F32)
        h_ref[pad:, :] = meta
        xn_ref[0:pad, :] = jnp.zeros((pad, h_ref.shape[1]), _BF16)
        xn_ref[pad:, :] = (_rms(meta) * g_ref[...]).astype(_BF16)

    @pl.when(t > 0)
    def _():
        x = x_ref[0]
        h_ref[...] = x
        xn_ref[...] = (_rms(x) * g_ref[...]).astype(_BF16)


def _embed(x, meta_tokens, g, *, tile):
    batch, seq, d = x.shape
    nt = seq // tile + 1
    return pl.pallas_call(
        _embed_kernel,
        grid=(batch, nt),
        in_specs=[
            pl.BlockSpec((1, tile, d), lambda b, t: (b, jnp.maximum(t - 1, 0), 0)),
            pl.BlockSpec((N_META, d), lambda b, t: (0, 0)),
            pl.BlockSpec((1, d), lambda b, t: (0, 0)),
        ],
        out_specs=[
            pl.BlockSpec((tile, d), lambda b, t: (b * nt + t, 0)),
            pl.BlockSpec((tile, d), lambda b, t: (b * nt + t, 0)),
        ],
        out_shape=[
            jax.ShapeDtypeStruct((batch * nt * tile, d), _F32),
            jax.ShapeDtypeStruct((batch * nt * tile, d), _BF16),
        ],
        compiler_params=_params(2),
        name="embed",
    )(x, meta_tokens, g)


def _a_in_kernel(xn_ref, wv_ref, wg_ref, wz_ref, bv_ref, bg_ref, bz_ref, y_ref, zs_ref,
                 *, rows_per_batch, front_pad):
    i = pl.program_id(0)
    x = xn_ref[...]
    tm = x.shape[0]
    pos = lax.rem(i * tm + lax.broadcasted_iota(jnp.int32, (tm, 1), 0), rows_per_batch)
    visible = pos >= front_pad
    half = wv_ref.shape[1] // 2
    for k in range(2):
        cs = slice(k * half, (k + 1) * half)
        val = _dot(x, wv_ref[:, cs]) + bv_ref[:, cs]
        gate = _dot(x, wg_ref[:, cs]) + bg_ref[:, cs]
        z = _dot(x, wz_ref[:, cs]) + bz_ref[:, cs]
        y_ref[:, cs] = jnp.where(visible, val * jax.nn.sigmoid(gate), 0.0).astype(_BF16)
        zs_ref[:, cs] = (z * jax.nn.sigmoid(z)).astype(_BF16)


def _a_in(xn, w_in, b_in, *, tile, col_tile, rows_per_batch, front_pad):
    rows, d = xn.shape
    e = w_in.shape[1] // 3
    nj = e // col_tile
    kern = functools.partial(_a_in_kernel, rows_per_batch=rows_per_batch, front_pad=front_pad)
    wspec = lambda k: pl.BlockSpec((d, col_tile), lambda i, j, k=k: (0, j + k * nj))
    bspec = lambda k: pl.BlockSpec((1, col_tile), lambda i, j, k=k: (0, j + k * nj))
    return pl.pallas_call(
        kern,
        grid=(rows // tile, nj),
        in_specs=[
            pl.BlockSpec((tile, d), lambda i, j: (i, 0)),
            wspec(0), wspec(1), wspec(2),
            bspec(0), bspec(1), bspec(2),
        ],
        out_specs=[
            pl.BlockSpec((tile, col_tile), lambda i, j: (i, j)),
            pl.BlockSpec((tile, col_tile), lambda i, j: (i, j)),
        ],
        out_shape=[
            jax.ShapeDtypeStruct((rows, e), _BF16),
            jax.ShapeDtypeStruct((rows, e), _BF16),
        ],
        compiler_params=_params(2),
        name="a_in",
    )(xn, w_in, w_in, w_in, b_in, b_in, b_in)


def _a_out_kernel(*refs, n_norms, emit_lo, conv_rows, norm_rows):
    (y_ref, halo_ref, zs_ref, h_ref, dww_ref, dwb_ref, lng_ref, lnb_ref, w_ref), refs = refs[:9], refs[9:]
    gain_refs, refs = refs[:n_norms], refs[n_norms:]
    o_ref, refs = refs[0], refs[1:]
    norm_refs, refs = refs[:n_norms], refs[n_norms:]
    lo_ref = None
    if emit_lo:
        lo_ref, refs = refs[0], refs[1:]
    shift_ref, conv_ref, u_ref = refs

    tm, e = y_ref.shape
    for c in range(e // 128):
        cs = pl.ds(c * 128, 128)
        strip = jnp.concatenate([halo_ref[:, cs], y_ref[:, cs]], axis=0).astype(_F32)
        shift_ref[0] = strip
        for b in range(1, 8):
            shift_ref[b] = pltpu.roll(strip, b, 0)

        def conv_body(rb, carry, cs=cs):
            r0 = pl.multiple_of(rb * conv_rows, conv_rows)
            acc = jnp.broadcast_to(dwb_ref[:, cs], (conv_rows, 128))
            for k in range(CONV_KERNEL):
                d = CONV_KERNEL - 1 - k
                src = shift_ref[d % 8, pl.ds(r0 + CONV_HALO - 8 * (d // 8), conv_rows), :]
                acc = acc + dww_ref[pl.ds(k, 1), cs] * src
            conv_ref[pl.ds(r0, conv_rows), cs] = acc
            return carry

        lax.fori_loop(0, tm // conv_rows, conv_body, 0)

    def norm_body(rb, carry):
        r0 = pl.multiple_of(rb * norm_rows, norm_rows)
        rs = pl.ds(r0, norm_rows)
        v = conv_ref[rs, :]
        mu = jnp.mean(v, axis=-1, keepdims=True)
        vc = v - mu
        var = jnp.mean(vc * vc, axis=-1, keepdims=True)
        yn = vc * lax.rsqrt(var + EPS) * lng_ref[...] + lnb_ref[...]
        u = yn * jax.nn.sigmoid(yn) * zs_ref[rs, :].astype(_F32)
        u_ref[rs, :] = u.astype(_BF16)
        return carry

    lax.fori_loop(0, tm // norm_rows, norm_body, 0)
    hn = h_ref[...] + _dot(u_ref[...], w_ref[...])
    o_ref[...] = hn
    _emit_norms(hn, gain_refs, norm_refs, lo_ref)


def _a_out(y, zs, h, dw_w, dw_b, ln_g, ln_b, w_out, gains, *, tile, emit_lo):
    rows, e = y.shape
    d = h.shape[1]
    halo_blocks = tile // CONV_HALO
    n_norms = len(gains)
    kern = functools.partial(_a_out_kernel, n_norms=n_norms, emit_lo=emit_lo,
                             conv_rows=min(A_CONV_ROWS, tile), norm_rows=min(A_NORM_ROWS, tile))
    row = lambda i: (i, 0)
    n_bf16_out = n_norms + (1 if emit_lo else 0)
    return pl.pallas_call(
        kern,
        grid=(rows // tile,),
        in_specs=[
            pl.BlockSpec((tile, e), row),
            pl.BlockSpec((CONV_HALO, e), lambda i: (jnp.maximum(i * halo_blocks - 1, 0), 0)),
            pl.BlockSpec((tile, e), row),
            pl.BlockSpec((tile, d), row),
            _resident((CONV_KERNEL, e)),
            _resident((1, e)),
            _resident((1, e)),
            _resident((1, e)),
            _resident((e, d)),
        ] + [_resident((1, d))] * n_norms,
        out_specs=[pl.BlockSpec((tile, d), row)] * (1 + n_bf16_out),
        out_shape=[jax.ShapeDtypeStruct((rows, d), _F32)]
        + [jax.ShapeDtypeStruct((rows, d), _BF16)] * n_bf16_out,
        scratch_shapes=[
            pltpu.VMEM((8, tile + CONV_HALO, 128), _F32),
            pltpu.VMEM((tile, e), _F32),
            pltpu.VMEM((tile, e), _BF16),
        ],
        compiler_params=_params(1),
        name="a_out",
    )(y, y, zs, h, dw_w, dw_b, ln_g, ln_b, w_out, *gains)


def _kv_kernel(xn_ref, xlo_ref, wk_ref, wvt_ref, wfh_ref, wfl_ref, bf_ref, sel_ref,
               kp_ref, vt_ref, ct_ref, ccol_ref, carry_ref,
               *, rows_per_batch, front_pad, heads_per_step):
    i = pl.program_id(0)
    tm = xn_ref.shape[0]
    nh = wfh_ref.shape[0]
    row0 = i * tm
    x = xn_ref[...]

    @pl.when(pl.program_id(1) == 0)
    def _():
        both = _dot_nt(jnp.concatenate([wfh_ref[...], wfl_ref[...]], axis=0), x)
        ft = both[0:nh] + _dot_nt(wfh_ref[...], xlo_ref[...]) + both[nh:2 * nh] + bf_ref[...]
        ls = jnp.minimum(ft, 0.0) - jnp.log1p(jnp.exp(-jnp.abs(ft)))
        pos = lax.rem(row0 + lax.broadcasted_iota(jnp.int32, (1, tm), 1), rows_per_batch)
        ls = jnp.where(pos >= front_pad, ls, 0.0)
        carry = jnp.where(lax.rem(row0, rows_per_batch) == 0, 0.0, carry_ref[...])
        tri = (lax.broadcasted_iota(jnp.int32, (tm, tm), 0)
               <= lax.broadcasted_iota(jnp.int32, (tm, tm), 1)).astype(_BF16)
        pieces = _dot(jnp.concatenate(_split3(ls), axis=0), tri)
        c = pieces[0:nh] + pieces[nh:2 * nh] + pieces[2 * nh:3 * nh] + carry
        carry_ref[...] = c[:, tm - 1:tm]
        hi, mid, lo = [p.astype(_F32) for p in _split3(c * LOG2E)]
        r = lax.broadcasted_iota(jnp.int32, (AUG_ROWS, tm), 0)
        ones_rows = jnp.where(r < 6, 1.0, 0.0)
        for hd in range(nh):
            blk = jnp.where(r == 0, hi[hd:hd + 1],
                            jnp.where(r == 1, mid[hd:hd + 1],
                                      jnp.where(r == 2, lo[hd:hd + 1], ones_rows)))
            ct_ref[hd] = blk.astype(_BF16)
        stack = jnp.concatenate([hi, mid, lo, jnp.zeros((128 - 3 * nh, tm), _F32)], axis=0)
        ccol_ref[...] = stack.T.astype(_BF16)

    kk = _dot(x, wk_ref[...])
    ka = _dot(ccol_ref[...], sel_ref[...])
    width = ka.shape[1]
    lane = lax.broadcasted_iota(jnp.int32, (tm, width), 1) & (HEAD_DIM - 1)
    pos = lax.rem(row0 + lax.broadcasted_iota(jnp.int32, (tm, 1), 0), rows_per_batch)
    ka = jnp.where(lane < 3, 1.0, ka)
    ka = jnp.where(jnp.logical_and(lane == 3, pos < front_pad), MASKED_KEY, ka)
    vt = _dot_nt(wvt_ref[...], x)
    ones_blk = jnp.where(lax.broadcasted_iota(jnp.int32, (BF16_SUBLANES, tm), 0) == 0, 1.0, 0.0).astype(_BF16)
    for hh in range(heads_per_step):
        hs = slice(hh * HEAD_DIM, (hh + 1) * HEAD_DIM)
        kp_ref[hh, :, 0:HEAD_DIM] = kk[:, hs].astype(_BF16)
        kp_ref[hh, :, HEAD_DIM:QK_DEPTH] = ka[:, hs].astype(_BF16)
        vt_ref[hh, 0, 0:HEAD_DIM, :] = vt[hs, :].astype(_BF16)
        vt_ref[hh, 0, HEAD_DIM:V_ROWS, :] = ones_blk


def _kv(xn, xlo, wk, wvt, wfh, wfl, bf, sel, *, tile, heads_per_step, rows_per_batch, front_pad):
    rows, d = xn.shape
    nh = wfh.shape[0]
    cw = heads_per_step * HEAD_DIM
    kern = functools.partial(_kv_kernel, rows_per_batch=rows_per_batch, front_pad=front_pad,
                             heads_per_step=heads_per_step)
    return pl.pallas_call(
        kern,
        grid=(rows // tile, nh // heads_per_step),
        in_specs=[
            pl.BlockSpec((tile, d), lambda i, j: (i, 0)),
            pl.BlockSpec((tile, d), lambda i, j: (i, 0)),
            pl.BlockSpec((d, cw), lambda i, j: (0, j)),
            pl.BlockSpec((cw, d), lambda i, j: (j, 0)),
            pl.BlockSpec((nh, d), lambda i, j: (0, 0)),
            pl.BlockSpec((nh, d), lambda i, j: (0, 0)),
            pl.BlockSpec((nh, 1), lambda i, j: (0, 0)),
            pl.BlockSpec((128, cw), lambda i, j: (0, j)),
        ],
        out_specs=[
            pl.BlockSpec((heads_per_step, tile, QK_DEPTH), lambda i, j: (j, i, 0)),
            pl.BlockSpec((heads_per_step, 1, V_ROWS, tile), lambda i, j: (j, i, 0, 0)),
            pl.BlockSpec((nh, AUG_ROWS, tile), lambda i, j: (0, 0, i)),
        ],
        out_shape=[
            jax.ShapeDtypeStruct((nh, rows, QK_DEPTH), _BF16),
            jax.ShapeDtypeStruct((nh, rows // tile, V_ROWS, tile), _BF16),
            jax.ShapeDtypeStruct((nh, AUG_ROWS, rows), _BF16),
        ],
        scratch_shapes=[
            pltpu.VMEM((tile, 128), _BF16),
            pltpu.VMEM((nh, 1), _F32),
        ],
        compiler_params=_params(2),
        name="kv_proj",
    )(xn, xlo, wk, wvt, wfh, wfl, bf, sel)


def _q_kernel(xn_ref, wqt_ref, wz_ref, qt_ref, zs_ref, *, heads_per_step, q_scale):
    x = xn_ref[...]
    hh2 = heads_per_step // 2
    cw2 = hh2 * HEAD_DIM
    for k in range(2):
        qt = _dot_nt(wqt_ref[k * cw2:(k + 1) * cw2, :], x) * q_scale
        for hh in range(hh2):
            qt_ref[k * hh2 + hh] = qt[hh * HEAD_DIM:(hh + 1) * HEAD_DIM, :].astype(_BF16)
        z = _dot(x, wz_ref[:, k * cw2:(k + 1) * cw2])
        zs_ref[:, k * cw2:(k + 1) * cw2] = (z * jax.nn.sigmoid(z)).astype(_BF16)


def _q_proj(xn, wqt, wz, *, tile, heads_per_step):
    rows, d = xn.shape
    a = wz.shape[1]
    nh = a // HEAD_DIM
    cw = heads_per_step * HEAD_DIM
    kern = functools.partial(_q_kernel, heads_per_step=heads_per_step,
                             q_scale=HEAD_DIM ** -0.5 * LOG2E)
    return pl.pallas_call(
        kern,
        grid=(rows // tile, nh // heads_per_step),
        in_specs=[
            pl.BlockSpec((tile, d), lambda i, j: (i, 0)),
            pl.BlockSpec((cw, d), lambda i, j: (j, 0)),
            pl.BlockSpec((d, cw), lambda i, j: (0, j)),
        ],
        out_specs=[
            pl.BlockSpec((heads_per_step, HEAD_DIM, tile), lambda i, j: (j, 0, i)),
            pl.BlockSpec((tile, cw), lambda i, j: (i, j)),
        ],
        out_shape=[
            jax.ShapeDtypeStruct((nh, HEAD_DIM, rows), _BF16),
            jax.ShapeDtypeStruct((rows, a), _BF16),
        ],
        compiler_params=_params(2),
        name="q_proj",
    )(xn, wqt, wz)


def _attn_kernel(qt_ref, ct_ref, k_ref, vt_ref, o_ref, acc_ref, m_ref, sa_ref, sb_ref, *, pairs_per_trip):
    nq = vt_ref.shape[1]
    tq = tk = vt_ref.shape[3]

    def q_aug(qi):
        cols = pl.ds(pl.multiple_of(qi * tq, tq), tq)
        return jnp.concatenate(
            [qt_ref[0, :, cols], ct_ref[0, :, cols],
             jnp.zeros((QK_DEPTH - HEAD_DIM - AUG_ROWS, tq), _BF16)], axis=0)

    def logits(qa, j, s_ref):
        ks = k_ref[0, pl.ds(pl.multiple_of(j * tk, tk), tk), :]
        s_ref[...] = _dot(ks, qa)

    def consume(j, s_ref, on_diagonal):
        s = s_ref[...]
        if on_diagonal:
            key = lax.broadcasted_iota(jnp.int32, (tk, tq), 0)
            qry = lax.broadcasted_iota(jnp.int32, (tk, tq), 1)
            s = jnp.where(key <= qry, s, -jnp.inf)
        m_prev = m_ref[...]
        m_new = jnp.maximum(m_prev, jnp.max(s, axis=0, keepdims=True))
        p = jnp.exp2(s - m_new)
        alpha = jnp.exp2(m_prev - m_new)
        acc_ref[...] = alpha * acc_ref[...] + _dot(vt_ref[0, j], p.astype(_BF16))
        m_ref[...] = m_new

    logits(q_aug(0), 0, sa_ref)
    trip = 2 * pairs_per_trip

    def q_block(qi, carry):
        qa = q_aug(qi)
        qa_next = q_aug(jnp.minimum(qi + 1, nq - 1))
        m_ref[...] = jnp.full(m_ref.shape, -jnp.inf, _F32)
        acc_ref[...] = jnp.zeros(acc_ref.shape, _F32)

        def pairs(j, n):
            for t in range(n):
                logits(qa, j + 2 * t + 1, sb_ref)
                consume(j + 2 * t, sa_ref, False)
                logits(qa, j + 2 * t + 2, sa_ref)
                consume(j + 2 * t + 1, sb_ref, False)

        n_trips = qi // trip

        def trip_body(t, c):
            pairs(t * trip, pairs_per_trip)
            return c

        lax.fori_loop(0, n_trips, trip_body, 0)
        done = n_trips * trip
        rem = qi - done
        half_trip = trip // 2

        @pl.when(rem >= half_trip)
        def _():
            pairs(done, pairs_per_trip // 2)

        done = done + (rem & half_trip)
        rem = rem & (half_trip - 1)

        def tail(r):
            bufs = (sa_ref, sb_ref)
            for t in range(r):
                logits(qa, done + t + 1, bufs[(t + 1) % 2])
                consume(done + t, bufs[t % 2], False)
            if r % 2 == 1:
                logits(qa_next, 0, sa_ref)
                consume(qi, sb_ref, True)
            else:
                consume(qi, sa_ref, True)
                logits(qa_next, 0, sa_ref)

        for r in range(half_trip):
            @pl.when(rem == r)
            def _(r=r):
                tail(r)

        acc = acc_ref[...]
        o = acc[0:HEAD_DIM, :] / acc[HEAD_DIM:HEAD_DIM + 1, :]
        o_ref[pl.ds(pl.multiple_of(qi * tq, tq), tq), :] = o.T.astype(_BF16)
        return carry

    lax.fori_loop(0, nq, q_block, 0)


def _attention(qt, ct, kp, vt, *, batch, tile):
    nh, _, rows = qt.shape
    nq = rows // tile // batch
    rows_per_batch = nq * tile
    return pl.pallas_call(
        functools.partial(_attn_kernel, pairs_per_trip=ATTN_PAIRS_PER_TRIP),
        grid=(nh, batch),
        in_specs=[
            pl.BlockSpec((1, HEAD_DIM, rows_per_batch), lambda h, b: (h, 0, b)),
            pl.BlockSpec((1, AUG_ROWS, rows_per_batch), lambda h, b: (h, 0, b)),
            pl.BlockSpec((1, rows_per_batch, QK_DEPTH), lambda h, b: (h, b, 0)),
            pl.BlockSpec((1, nq, V_ROWS, tile), lambda h, b: (h, b, 0, 0)),
        ],
        out_specs=pl.BlockSpec((rows_per_batch, HEAD_DIM), lambda h, b: (b, h)),
        out_shape=jax.ShapeDtypeStruct((rows, nh * HEAD_DIM), _BF16),
        scratch_shapes=[
            pltpu.VMEM((V_ROWS, tile), _F32),
            pltpu.VMEM((1, tile), _F32),
            pltpu.VMEM((tile, tile), _F32),
            pltpu.VMEM((tile, tile), _F32),
        ],
        compiler_params=_params(2),
        name="fox_attention",
    )(qt, ct, kp, vt)


def _b_out_kernel(o_ref, zs_ref, h_ref, w_ref, g_ref, out_ref, xn_ref):
    hn = h_ref[...] + _dot(o_ref[...] * zs_ref[...], w_ref[...])
    out_ref[...] = hn
    _emit_norms(hn, [g_ref], [xn_ref], None)


def _b_out_final_kernel(o_ref, zs_ref, h_ref, w_ref, g_ref, out_ref):
    hn = h_ref[...] + _dot(o_ref[...] * zs_ref[...], w_ref[...])
    out_ref[...] = _rms(hn) * g_ref[...]


def _b_out(o, zs, h, w_out, g, *, tile):
    rows, a = o.shape
    d = h.shape[1]
    row = lambda i: (i, 0)
    return pl.pallas_call(
        _b_out_kernel,
        grid=(rows // tile,),
        in_specs=[
            pl.BlockSpec((tile, a), row),
            pl.BlockSpec((tile, a), row),
            pl.BlockSpec((tile, d), row),
            _resident((a, d)),
            _resident((1, d)),
        ],
        out_specs=[pl.BlockSpec((tile, d), row), pl.BlockSpec((tile, d), row)],
        out_shape=[jax.ShapeDtypeStruct((rows, d), _F32), jax.ShapeDtypeStruct((rows, d), _BF16)],
        compiler_params=_params(1),
        name="b_out",
    )(o, zs, h, w_out, g)


def _b_out_final(o, zs, h, w_out, g, *, tile, batch, seq):
    rows, a = o.shape
    d = h.shape[1]
    tiles_per_batch = rows // batch // tile
    out_tiles_per_batch = seq // tile
    row = lambda i: (i, 0)

    def out_map(i):
        b = i // tiles_per_batch
        t = i - b * tiles_per_batch
        return (b * out_tiles_per_batch + jnp.maximum(t - (tiles_per_batch - out_tiles_per_batch), 0), 0)

    return pl.pallas_call(
        _b_out_final_kernel,
        grid=(rows // tile,),
        in_specs=[
            pl.BlockSpec((tile, a), row),
            pl.BlockSpec((tile, a), row),
            pl.BlockSpec((tile, d), row),
            _resident((a, d)),
            _resident((1, d)),
        ],
        out_specs=pl.BlockSpec((tile, d), out_map),
        out_shape=jax.ShapeDtypeStruct((batch * seq, d), _F32),
        compiler_params=_params(1),
        name="b_out_final",
    )(o, zs, h, w_out, g)


def _forward(x, meta_tokens, a_norm_g, a_w_in, a_b_in, a_dw_w, a_dw_b, a_ln_g, a_ln_b, a_w_out,
             kv_norm_g, w_kvf, b_f, b_norm_g, b_w_in, b_w_out, final_norm_g,
             *, tile, a_in_rows, a_out_rows):
    batch, seq, d = x.shape
    n_a = a_w_in.shape[0]
    n_b = b_w_in.shape[0]
    attn_width = b_w_out.shape[1]
    nh = attn_width // HEAD_DIM
    heads_per_step = min(4, nh)
    assert seq % tile == 0 and N_META <= tile and a_out_rows % CONV_HALO == 0
    assert nh % heads_per_step == 0 and heads_per_step % 2 == 0 and 3 * nh <= 128 and n_a >= 1 and n_b >= 1
    rows_per_batch = seq + tile
    front_pad = tile - N_META
    rows = batch * rows_per_batch
    assert rows % a_in_rows == 0 and rows % a_out_rows == 0
    col_tile = min(512, a_w_in.shape[2] // 3)

    row_vec = lambda v: v.reshape(1, -1).astype(_F32)
    x = x.astype(_F32)
    h, xn = _embed(x, meta_tokens.astype(_F32), row_vec(a_norm_g[0]), tile=tile)

    xn_kv = xlo_kv = None
    for layer in range(n_a):
        y, zs = _a_in(xn, a_w_in[layer].astype(_BF16), row_vec(a_b_in[layer]), tile=a_in_rows,
                      col_tile=col_tile, rows_per_batch=rows_per_batch, front_pad=front_pad)
        last = layer + 1 == n_a
        gains = [row_vec(kv_norm_g), row_vec(b_norm_g[0])] if last else [row_vec(a_norm_g[layer + 1])]
        outs = _a_out(y, zs, h, a_dw_w[layer].astype(_F32), row_vec(a_dw_b[layer]), row_vec(a_ln_g[layer]),
                      row_vec(a_ln_b[layer]), a_w_out[layer].astype(_BF16), gains,
                      tile=a_out_rows, emit_lo=last)
        if last:
            h, xn_kv, xn, xlo_kv = outs
        else:
            h, xn = outs

    wk = w_kvf[:, :attn_width].astype(_BF16)
    wvt = w_kvf[:, attn_width:2 * attn_width].T.astype(_BF16)
    wft = w_kvf[:, 2 * attn_width:].T.astype(_F32)
    wfh = wft.astype(_BF16)
    wfl = (wft - wfh.astype(_F32)).astype(_BF16)
    piece = jnp.arange(128)[:, None]
    col = jnp.arange(attn_width)[None, :]
    sel = jnp.where((piece < 3 * nh) & (col // HEAD_DIM == piece % nh)
                    & (col % HEAD_DIM == 3 + piece // nh), -1.0, 0.0).astype(_BF16)
    kp, vt, ct = _kv(xn_kv, xlo_kv, wk, wvt, wfh, wfl, b_f.reshape(nh, 1).astype(_F32), sel,
                     tile=tile, heads_per_step=heads_per_step,
                     rows_per_batch=rows_per_batch, front_pad=front_pad)

    for layer in range(n_b):
        wqt = b_w_in[layer][:, :attn_width].T.astype(_BF16)
        wz = b_w_in[layer][:, attn_width:].astype(_BF16)
        qt, zs = _q_proj(xn, wqt, wz, tile=a_in_rows, heads_per_step=heads_per_step)
        o = _attention(qt, ct, kp, vt, batch=batch, tile=tile)
        w_out = b_w_out[layer].astype(_BF16)
        if layer + 1 < n_b:
            h, xn = _b_out(o, zs, h, w_out, row_vec(b_norm_g[layer + 1]), tile=tile)
        else:
            h = _b_out_final(o, zs, h, w_out, row_vec(final_norm_g), tile=tile, batch=batch, seq=seq)
    return h.reshape(batch, seq, d)


def kernel(x, meta_tokens, a_norm_g, a_w_in, a_b_in, a_dw_w, a_dw_b, a_ln_g, a_ln_b, a_w_out,
           kv_norm_g, w_kvf, b_f, b_norm_g, b_w_in, b_w_out, final_norm_g):
    return _forward(x, meta_tokens, a_norm_g, a_w_in, a_b_in, a_dw_w, a_dw_b, a_ln_g, a_ln_b, a_w_out,
                    kv_norm_g, w_kvf, b_f, b_norm_g, b_w_in, b_w_out, final_norm_g,
                    tile=SEQ_TILE, a_in_rows=A_IN_ROWS, a_out_rows=A_OUT_ROWS)
```
